```python
import jax
import jax.numpy as jnp
from jax import lax
import numpy as np

D_MODEL = 1024
BATCH = 4
SEQ = 8192
DEPTH = 2
DEC_BATCH = 32
DEC_SEQ = 4
PAST_LEN = 16384
PAGE_SIZE = 128

N_HEADS = 8
HEAD_DIM = 64
N_KV = 2
Q_PER_KV = N_HEADS // N_KV
L_CMP = 32
S_CMP = 16
CMP_HIDDEN = 128
L_SEL = 64
N_SEL_TOP = 16
WINDOW = 512
Q_BLOCK = 128
FORCE_BONUS = 1.0e4
CONV_CH = 512
CONV_W = 31
N_MEM = 256
X_HEADS = 4
X_HEAD_DIM = 128
N_GROUPS = 4
EXP_PER_GROUP = 8
N_EXPERTS = N_GROUPS * EXP_PER_GROUP
TOP_IN_GROUP = 2
D_EXPERT = 512
MOE_BLOCK = 128
EPS = 1e-6
NSA_Q = N_HEADS * HEAD_DIM
NSA_KV = 2 * N_KV * HEAD_DIM
X_Q = X_HEADS * X_HEAD_DIM
N_IN = NSA_Q + 3 * NSA_KV + 3 * N_HEADS + 2 * CONV_CH + X_Q
N_BRANCH = NSA_Q + CONV_CH + X_Q

kernel_name = 'nsa_conformer_memory_hmoe_decode_step'


def rmsnorm(x, g):
    xf = x.astype(jnp.float32)
    y = xf * lax.rsqrt(jnp.mean(xf * xf, axis=-1, keepdims=True) + EPS)
    return (y * g.astype(jnp.float32)).astype(x.dtype)


def layernorm(x, g, b):
    xf = x.astype(jnp.float32)
    xc = xf - jnp.mean(xf, axis=-1, keepdims=True)
    y = xc * lax.rsqrt(jnp.mean(xc * xc, axis=-1, keepdims=True) + EPS)
    return (y * g.astype(jnp.float32) + b.astype(jnp.float32)).astype(x.dtype)


def masked_softmax(s, mask):
    s = jnp.where(mask, s, -jnp.inf)
    m = jnp.max(s, axis=-1, keepdims=True)
    m = jnp.where(jnp.isfinite(m), m, 0.0)
    e = jnp.where(mask, jnp.exp(s - m), 0.0)
    return e / jnp.maximum(jnp.sum(e, axis=-1, keepdims=True), jnp.finfo(jnp.float32).tiny)


def alibi_slopes():
    h = jnp.arange(N_HEADS, dtype=jnp.float32) + 1.0
    return jnp.exp2(-8.0 * h / N_HEADS).reshape(N_KV, Q_PER_KV)


def alibi_bias(dist, slopes):
    return -slopes[None, :, :, None] * dist[:, None, None, :].astype(jnp.float32)


def norm_keys(kv, g):
    return jnp.stack([rmsnorm(kv[:, :, 0], g), kv[:, :, 1]], axis=2)


def mixer_inputs(h, w_in, q_gain, k_gain, xq_gain):
    B, T, _ = h.shape
    sizes = [NSA_Q, NSA_KV, NSA_KV, NSA_KV, 3 * N_HEADS, 2 * CONV_CH, X_Q]
    cuts = [int(c) for c in np.cumsum(sizes)[:-1]]
    q, kv_c, kv_s, kv_w, g, glu, qx = jnp.split(h @ w_in, cuts, axis=-1)
    kv_shape = (B, T, 2, N_KV, HEAD_DIM)
    q = rmsnorm(q.reshape(B, T, N_HEADS, HEAD_DIM), q_gain)
    kv_c = kv_c.reshape(kv_shape)
    kv_s = norm_keys(kv_s.reshape(kv_shape), k_gain[1])
    kv_w = norm_keys(kv_w.reshape(kv_shape), k_gain[2])
    gates = jax.nn.sigmoid(g).reshape(B, T, N_HEADS, 3)
    u = glu[..., :CONV_CH] * jax.nn.sigmoid(glu[..., CONV_CH:])
    qx = rmsnorm(qx.reshape(B, T, X_HEADS, X_HEAD_DIM), xq_gain)
    return q, kv_c, kv_s, kv_w, gates, u, qx


def compress(rows, w1, w2, pos):
    B, T, G, Dh = rows.shape
    n_chunk = T // S_CMP
    ch = rows[:, :n_chunk * S_CMP].reshape(B, n_chunk, S_CMP, G, Dh)
    first = jnp.einsum('bcpgd,pdf->bcgf', ch, w1[:S_CMP])
    second = jnp.einsum('bcpgd,pdf->bcgf', ch, w1[S_CMP:])
    pos_bias = jnp.einsum('pd,pdf->f', pos, w1)
    hid = jax.nn.gelu(first[:, :-1] + second[:, 1:] + pos_bias)
    return hid @ w2


def sel_blocks(rows, n_sel):
    B, T, G, Dh = rows.shape
    rows = jnp.pad(rows, ((0, 0), (0, n_sel * L_SEL - T), (0, 0), (0, 0)))
    return rows.reshape(B, n_sel, L_SEL, G, Dh).transpose(0, 3, 1, 2, 4)


def sel_cover(n_cmp, n_sel):
    start = jnp.arange(n_cmp)[:, None] * S_CMP
    j = jnp.arange(n_sel)[None, :]
    return ((start < (j + 1) * L_SEL) & (start + L_CMP > j * L_SEL)).astype(jnp.float32)


def nsa_block(q, q_pos, gates, k_cmp, v_cmp, k_blk, v_blk, k_win, v_win, win_pos, cover):
    B, Q, _, Dh = q.shape
    f32 = jnp.float32
    scale = Dh ** -0.5
    slopes = alibi_slopes()
    qg = q.reshape(B, Q, N_KV, Q_PER_KV, Dh)
    n_cmp = k_cmp.shape[1]
    d_c = q_pos[:, None] - (jnp.arange(n_cmp) * S_CMP + (L_CMP - 1))[None, :]
    s_c = jnp.einsum('bqgrd,bngd->bqgrn', qg, k_cmp, preferred_element_type=f32) * scale + alibi_bias(d_c, slopes)
    p_c = masked_softmax(s_c, (d_c >= 0)[:, None, None, :])
    o_c = jnp.einsum('bqgrn,bngd->bqgrd', p_c.astype(v_cmp.dtype), v_cmp)
    n_sel = k_blk.shape[2]
    n_top = min(N_SEL_TOP, n_sel)
    imp = jnp.einsum('bqgrn,ns->bqgs', p_c, cover)
    blk = jnp.arange(n_sel)[None, :]
    cur = (q_pos // L_SEL)[:, None]
    valid = blk * L_SEL <= q_pos[:, None]
    forced = (blk == 0) | (blk == cur) | (blk == cur - 1)
    score = jnp.where(valid[:, None, :], imp + jnp.where(forced, FORCE_BONUS, 0.0)[:, None, :], -FORCE_BONUS)
    top_s, idx = lax.top_k(score, n_top)
    bi = jnp.arange(B)[:, None, None, None]
    gi = jnp.arange(N_KV)[None, None, :, None]
    k_sel = k_blk[bi, gi, idx]
    v_sel = v_blk[bi, gi, idx]
    d_s = q_pos[None, :, None, None, None] - (idx[..., None] * L_SEL + jnp.arange(L_SEL))
    m_s = (d_s >= 0) & (top_s > -0.5 * FORCE_BONUS)[..., None]
    s_s = jnp.einsum('bqgrd,bqgkld->bqgrkl', qg, k_sel, preferred_element_type=f32) * scale
    s_s = s_s - slopes[None, None, :, :, None, None] * d_s[:, :, :, None].astype(f32)
    p_s = masked_softmax(s_s.reshape(B, Q, N_KV, Q_PER_KV, -1), m_s.reshape(B, Q, N_KV, 1, -1)).reshape(s_s.shape)
    o_s = jnp.einsum('bqgrkl,bqgkld->bqgrd', p_s.astype(v_sel.dtype), v_sel)
    d_w = q_pos[:, None] - win_pos[None, :]
    m_w = (d_w >= 0) & (d_w < WINDOW) & (win_pos >= 0)[None, :]
    s_w = jnp.einsum('bqgrd,bsgd->bqgrs', qg, k_win, preferred_element_type=f32) * scale + alibi_bias(d_w, slopes)
    p_w = masked_softmax(s_w, m_w[:, None, None, :])
    o_w = jnp.einsum('bqgrs,bsgd->bqgrd', p_w.astype(v_win.dtype), v_win)
    g = gates.reshape(B, Q, N_KV, Q_PER_KV, 3)
    o = g[..., 0:1] * o_c + g[..., 1:2] * o_s + g[..., 2:3] * o_w
    return o.reshape(B, Q, N_HEADS * Dh)


def nsa_prompt(q, gates, kv_c, kv_s, kv_w, w1, w2, pos, kg_cmp):
    B, T = q.shape[:2]
    k_cmp = rmsnorm(compress(kv_c[:, :, 0], w1[0], w2[0], pos[0]), kg_cmp)
    v_cmp = compress(kv_c[:, :, 1], w1[1], w2[1], pos[1])
    n_sel = -(-T // L_SEL)
    k_blk = sel_blocks(kv_s[:, :, 0], n_sel)
    v_blk = sel_blocks(kv_s[:, :, 1], n_sel)
    cover = sel_cover(k_cmp.shape[1], n_sel)
    win = jnp.pad(kv_w, ((0, 0), (WINDOW, 0), (0, 0), (0, 0), (0, 0)))
    n_qb = T // Q_BLOCK
    qb = q.reshape(B, n_qb, Q_BLOCK, N_HEADS, HEAD_DIM).swapaxes(0, 1)
    gb = gates.reshape(B, n_qb, Q_BLOCK, N_HEADS, 3).swapaxes(0, 1)

    def one_block(args):
        i, q_i, g_i = args
        start = i * Q_BLOCK
        w_i = lax.dynamic_slice_in_dim(win, start, WINDOW + Q_BLOCK, axis=1)
        win_pos = start - WINDOW + jnp.arange(WINDOW + Q_BLOCK)
        q_pos = start + jnp.arange(Q_BLOCK)
        return nsa_block(q_i, q_pos, g_i, k_cmp, v_cmp, k_blk, v_blk, w_i[:, :, 0], w_i[:, :, 1], win_pos, cover)

    out = lax.map(one_block, (jnp.arange(n_qb), qb, gb))
    return out.swapaxes(0, 1).reshape(B, T, NSA_Q)


def nsa_sample(q, gates, kv_c, kv_s, kv_w, cmp_pages, sel_pages, win_buf, page_table, w1, w2, pos, kg_cmp):
    DB, DS = q.shape[:2]
    past = page_table.shape[1] * PAGE_SIZE

    def gather(pages):
        return pages[page_table].reshape((DB, past) + pages.shape[2:])

    full_c = jnp.concatenate([gather(cmp_pages), kv_c], axis=1)
    full_s = jnp.concatenate([gather(sel_pages), kv_s], axis=1)
    T = past + DS
    k_cmp = rmsnorm(compress(full_c[:, :, 0], w1[0], w2[0], pos[0]), kg_cmp)
    v_cmp = compress(full_c[:, :, 1], w1[1], w2[1], pos[1])
    n_sel = -(-T // L_SEL)
    win = jnp.concatenate([win_buf, kv_w], axis=1)
    n_buf = win_buf.shape[1]
    win_pos = past - n_buf + jnp.arange(n_buf + DS)
    o = nsa_block(q, past + jnp.arange(DS), gates, k_cmp, v_cmp,
                  sel_blocks(full_s[:, :, 0], n_sel), sel_blocks(full_s[:, :, 1], n_sel),
                  win[:, :, 0], win[:, :, 1], win_pos, sel_cover(k_cmp.shape[1], n_sel))
    return o, win[:, DS:]


def conv_module(u_ext, w, b, ln_g, ln_b):
    c = lax.conv_general_dilated(u_ext, w[:, None, :], window_strides=(1,), padding='VALID',
                                 dimension_numbers=('NWC', 'WIO', 'NWC'), feature_group_count=CONV_CH) + b
    return jax.nn.silu(layernorm(c, ln_g, ln_b))


def memory_kv(mem, norm_g, w_mem, k_gain):
    B, M, _ = mem.shape
    kv = (rmsnorm(mem, norm_g) @ w_mem).reshape(B, M, 2, X_HEADS, X_HEAD_DIM)
    return norm_keys(kv, k_gain)


def mem_attend(qx, mem_kv):
    B, T = qx.shape[:2]
    s = jnp.einsum('bthd,bmhd->bthm', qx, mem_kv[:, :, 0], preferred_element_type=jnp.float32) * X_HEAD_DIM ** -0.5
    p = jax.nn.softmax(s, axis=-1)
    o = jnp.einsum('bthm,bmhd->bthd', p.astype(mem_kv.dtype), mem_kv[:, :, 1])
    return o.reshape(B, T, X_Q)


def expert_dispatch(x, e_id, e_w, w_up, w_down):
    N, D = x.shape
    M = e_id.shape[0]
    tok = jnp.arange(M) // TOP_IN_GROUP
    order = jnp.argsort(e_id)
    e_s, tok_s, w_s = e_id[order], tok[order], e_w[order]
    counts = jnp.bincount(e_id, length=N_EXPERTS)
    padded = (counts + MOE_BLOCK - 1) // MOE_BLOCK * MOE_BLOCK
    start = jnp.cumsum(counts) - counts
    pend = jnp.cumsum(padded)
    dest = (pend - padded)[e_s] + jnp.arange(M) - start[e_s]
    n_blk = -(-M // MOE_BLOCK) + N_EXPERTS
    xb = jnp.zeros((n_blk * MOE_BLOCK, D), x.dtype).at[dest].set(x[tok_s])
    blk_e = jnp.clip(jnp.searchsorted(pend, jnp.arange(n_blk) * MOE_BLOCK, side='right'), 0, N_EXPERTS - 1)

    def run(args):
        xi, e = args
        a, b = jnp.split(xi @ w_up[e], 2, axis=-1)
        return (jax.nn.silu(a) * b) @ w_down[e]

    yb = lax.map(run, (xb.reshape(n_blk, MOE_BLOCK, D), blk_e)).reshape(n_blk * MOE_BLOCK, D)
    return jnp.zeros((N, D), x.dtype).at[tok_s].add(yb[dest] * w_s[:, None].astype(x.dtype))


def hier_moe(h, w_grp, b_grp, w_route, b_route, w_up, w_down):
    B, T, D = h.shape
    x = h.reshape(B * T, D)
    N = x.shape[0]
    grp_p = jax.nn.softmax((x @ w_grp + b_grp).astype(jnp.float32), axis=-1)
    g_sel = jnp.argmax(grp_p, axis=-1)
    p_g = jnp.max(grp_p, axis=-1)
    e_logit = (x @ w_route + b_route).astype(jnp.float32).reshape(N, N_GROUPS, EXP_PER_GROUP)
    e_logit = e_logit[jnp.arange(N), g_sel]
    top_l, top_i = lax.top_k(e_logit, TOP_IN_GROUP)
    w = jax.nn.softmax(top_l, axis=-1) * p_g[:, None]
    expert = g_sel[:, None] * EXP_PER_GROUP + top_i
    y = expert_dispatch(x, expert.reshape(-1), w.reshape(-1), w_up, w_down)
    return y.reshape(B, T, D)


def layer_tail(x, h, o_nsa, o_conv, o_mem, w_branch, w_gate, b_gate, w_out, norm_ffn,
               w_grp, b_grp, w_route, b_route, w_up, w_down):
    y_nsa = o_nsa @ w_branch[:NSA_Q]
    y_conv = o_conv @ w_branch[NSA_Q:NSA_Q + CONV_CH]
    y_mem = o_mem @ w_branch[NSA_Q + CONV_CH:]
    g_nsa, g_conv, g_mem = jnp.split(jax.nn.sigmoid(h @ w_gate + b_gate), 3, axis=-1)
    x = x + (g_nsa * y_nsa + g_conv * y_conv + g_mem * y_mem) @ w_out
    return x + hier_moe(rmsnorm(x, norm_ffn), w_grp, b_grp, w_route, b_route, w_up, w_down)


def setup_inputs(seed: int = 0) -> dict:
    key = jax.random.key(seed)
    keys = iter(jax.random.split(key, 48))

    def nrm(shape, scale):
        return jax.random.normal(next(keys), shape, jnp.float32) * scale

    n_pages = PAST_LEN // PAGE_SIZE
    n_used = DEC_BATCH * n_pages
    n_phys = n_used + max(1, n_used // 4)
    win_buf = min(WINDOW, PAST_LEN)
    page_table = jax.random.permutation(next(keys), n_phys)[:n_used].reshape(DEC_BATCH, n_pages).astype(jnp.int32)
    return {
        'x_prompt': nrm((BATCH, SEQ, D_MODEL), 1.0),
        'x_sample': nrm((DEC_BATCH, DEC_SEQ, D_MODEL), 1.0),
        'cache_cmp_kv': nrm((DEPTH, n_phys, PAGE_SIZE, 2, N_KV, HEAD_DIM), 1.0),
        'cache_sel_kv': nrm((DEPTH, n_phys, PAGE_SIZE, 2, N_KV, HEAD_DIM), 1.0),
        'state_win_kv': nrm((DEPTH, DEC_BATCH, win_buf, 2, N_KV, HEAD_DIM), 1.0),
        'state_conv': nrm((DEPTH, DEC_BATCH, CONV_W - 1, CONV_CH), 1.0),
        'cache_mem_kv': nrm((DEPTH, DEC_BATCH, N_MEM, 2, X_HEADS, X_HEAD_DIM), 1.0),
        'page_table': page_table,
        'mem_prompt': nrm((BATCH, N_MEM, D_MODEL), 1.0),
        'norm_mix': 1.0 + nrm((DEPTH, D_MODEL), 0.05),
        'norm_ffn': 1.0 + nrm((DEPTH, D_MODEL), 0.05),
        'w_in': nrm((DEPTH, D_MODEL, N_IN), D_MODEL ** -0.5),
        'q_gain': 1.0 + nrm((DEPTH, HEAD_DIM), 0.05),
        'k_gain': 1.0 + nrm((DEPTH, 3, HEAD_DIM), 0.05),
        'w_cmp1': nrm((DEPTH, 2, L_CMP, HEAD_DIM, CMP_HIDDEN), (L_CMP * HEAD_DIM) ** -0.5),
        'w_cmp2': nrm((DEPTH, 2, CMP_HIDDEN, HEAD_DIM), CMP_HIDDEN ** -0.5),
        'pos_cmp': nrm((DEPTH, 2, L_CMP, HEAD_DIM), 0.1),
        'conv_w': nrm((DEPTH, CONV_W, CONV_CH), CONV_W ** -0.5),
        'conv_b': nrm((DEPTH, CONV_CH), 0.01),
        'conv_ln_g': 1.0 + nrm((DEPTH, CONV_CH), 0.05),
        'conv_ln_b': nrm((DEPTH, CONV_CH), 0.01),
        'mem_norm': 1.0 + nrm((DEPTH, D_MODEL), 0.05),
        'w_mem_kv': nrm((DEPTH, D_MODEL, 2 * X_Q), D_MODEL ** -0.5),
        'mem_qk_gain': 1.0 + nrm((DEPTH, 2, X_HEAD_DIM), 0.05),
        'w_branch': nrm((DEPTH, N_BRANCH, D_MODEL), NSA_Q ** -0.5),
        'w_gate': nrm((DEPTH, D_MODEL, 3 * D_MODEL), D_MODEL ** -0.5),
        'b_gate': nrm((DEPTH, 3 * D_MODEL), 0.01),
        'w_out': nrm((DEPTH, D_MODEL, D_MODEL), D_MODEL ** -0.5),
        'w_grp': nrm((DEPTH, D_MODEL, N_GROUPS), D_MODEL ** -0.5),
        'b_grp': nrm((DEPTH, N_GROUPS), 0.01),
        'w_route': nrm((DEPTH, D_MODEL, N_EXPERTS), D_MODEL ** -0.5),
        'b_route': nrm((DEPTH, N_EXPERTS), 0.01),
        'w_up': nrm((DEPTH, N_EXPERTS, D_MODEL, 2 * D_EXPERT), D_MODEL ** -0.5),
        'w_down': nrm((DEPTH, N_EXPERTS, D_EXPERT, D_MODEL), D_EXPERT ** -0.5),
    }


def reference(x_prompt, x_sample, cache_cmp_kv, cache_sel_kv, state_win_kv, state_conv, cache_mem_kv,
              page_table, mem_prompt, norm_mix, norm_ffn, w_in, q_gain, k_gain, w_cmp1, w_cmp2, pos_cmp,
              conv_w, conv_b, conv_ln_g, conv_ln_b, mem_norm, w_mem_kv, mem_qk_gain, w_branch, w_gate,
              b_gate, w_out, w_grp, b_grp, w_route, b_route, w_up, w_down):
    xp, xs = x_prompt, x_sample
    T = xp.shape[1]
    n_win_p = min(WINDOW, T)
    p_cmp, p_sel, p_win, p_conv, p_mem = [], [], [], [], []
    s_cmp, s_sel, s_win, s_conv = [], [], [], []
    for l in range(DEPTH):
        tail_w = (w_branch[l], w_gate[l], b_gate[l], w_out[l], norm_ffn[l],
                  w_grp[l], b_grp[l], w_route[l], b_route[l], w_up[l], w_down[l])
        h = rmsnorm(xp, norm_mix[l])
        q, kv_c, kv_s, kv_w, gates, u, qx = mixer_inputs(h, w_in[l], q_gain[l], k_gain[l], mem_qk_gain[l, 0])
        o_nsa = nsa_prompt(q, gates, kv_c, kv_s, kv_w, w_cmp1[l], w_cmp2[l], pos_cmp[l], k_gain[l, 0])
        u_ext = jnp.pad(u, ((0, 0), (CONV_W - 1, 0), (0, 0)))
        o_conv = conv_module(u_ext, conv_w[l], conv_b[l], conv_ln_g[l], conv_ln_b[l])
        m_kv = memory_kv(mem_prompt, mem_norm[l], w_mem_kv[l], mem_qk_gain[l, 1])
        o_mem = mem_attend(qx, m_kv)
        xp = layer_tail(xp, h, o_nsa, o_conv, o_mem, *tail_w)
        p_cmp.append(kv_c)
        p_sel.append(kv_s)
        p_win.append(kv_w[:, T - n_win_p:])
        p_conv.append(u[:, T - (CONV_W - 1):])
        p_mem.append(m_kv)
        h = rmsnorm(xs, norm_mix[l])
        q, kv_c, kv_s, kv_w, gates, u, qx = mixer_inputs(h, w_in[l], q_gain[l], k_gain[l], mem_qk_gain[l, 0])
        o_nsa, win_new = nsa_sample(q, gates, kv_c, kv_s, kv_w, cache_cmp_kv[l], cache_sel_kv[l], state_win_kv[l],
                                    page_table, w_cmp1[l], w_cmp2[l], pos_cmp[l], k_gain[l, 0])
        u_ext = jnp.concatenate([state_conv[l], u], axis=1)
        o_conv = conv_module(u_ext, conv_w[l], conv_b[l], conv_ln_g[l], conv_ln_b[l])
        o_mem = mem_attend(qx, cache_mem_kv[l])
        xs = layer_tail(xs, h, o_nsa, o_conv, o_mem, *tail_w)
        s_cmp.append(kv_c)
        s_sel.append(kv_s)
        s_win.append(win_new)
        s_conv.append(u_ext[:, -(CONV_W - 1):])
    return (xp, xs, jnp.stack(p_cmp), jnp.stack(p_sel), jnp.stack(p_win), jnp.stack(p_conv), jnp.stack(p_mem),
            jnp.stack(s_cmp), jnp.stack(s_sel), jnp.stack(s_win), jnp.stack(s_conv))
```

```python
import functools

import numpy as np
import jax
import jax.numpy as jnp
from jax import lax
from jax.experimental import pallas as pl
from jax.experimental.pallas import tpu as pltpu

F32 = jnp.float32
BF16 = jnp.bfloat16

D_MODEL = 1024
N_HEADS = 8
HEAD_DIM = 64
N_KV = 2
Q_PER_KV = 4
L_CMP = 32
S_CMP = 16
CMP_HIDDEN = 128
L_SEL = 64
N_SEL_TOP = 16
WINDOW = 512
Q_BLOCK = 128
FORCE_BONUS = 1.0e4
CONV_CH = 512
CONV_W = 31
X_HEADS = 4
X_HEAD_DIM = 128
N_GROUPS = 4
EXP_PER_GROUP = 8
N_EXPERTS = 32
TOP_IN_GROUP = 2
D_EXPERT = 512
PAGE_SIZE = 128
EPS = 1e-6

LANES = 128
NEG = -1e30
TINY = float(np.finfo(np.float32).tiny)
VMEM_LIMIT = 56 * 1024 * 1024

C_Q, C_KVC, C_KVS, C_KVW, C_GLU_A, C_GLU_B, C_QX, C_GATE, C_END = 0, 512, 768, 1024, 1280, 1792, 2304, 2816, 2944

TM_PROJ = 256
TM_TAIL = 256
SEL_CHUNK = 512
CMP_PAGES = 32
SAMPLE_PAGES = 16
MOE_BLOCK = 128
T_SAMPLE_PAD = 8


def _cparams(*sem):
    return pltpu.CompilerParams(dimension_semantics=sem, vmem_limit_bytes=VMEM_LIMIT)


def _dot(a, b):
    return jnp.dot(a, b, preferred_element_type=F32)


def _dot_nt(a, b):
    return lax.dot_general(a, b, (((1,), (1,)), ((), ())), preferred_element_type=F32)


def _split(a):
    hi = a.astype(BF16)
    lo = (a - hi.astype(F32)).astype(BF16)
    return hi, lo


def _dot_hl(a, s):
    hi, lo = _split(a)
    return _dot(hi, s) + _dot(lo, s)


def _rms(x, g):
    return x * lax.rsqrt(jnp.mean(x * x, axis=-1, keepdims=True) + EPS) * g


def _segnorm64(v, seg, segt, gain):
    ss = _dot_hl(v * v, seg)
    inv = lax.rsqrt(ss * (1.0 / HEAD_DIM) + EPS)
    return v * _dot_hl(inv, segt) * gain


def _masked_softmax(s, mask):
    sm = jnp.where(mask, s, NEG)
    m = jnp.max(sm, axis=-1, keepdims=True)
    e = jnp.where(mask, jnp.exp(sm - m), 0.0)
    return e / jnp.maximum(jnp.sum(e, axis=-1, keepdims=True), TINY)


def _iota(shape, dim):
    return lax.broadcasted_iota(jnp.int32, shape, dim)


def _seg_mats(width):
    n = width // HEAD_DIM
    s = np.zeros((width, LANES), np.float32)
    s[np.arange(width), np.arange(width) // HEAD_DIM] = 1.0
    assert n <= LANES
    return jnp.asarray(s, BF16), jnp.asarray(s.T.copy(), BF16)


def _proj_in_kernel(x_ref, nm_ref, w_ref, qg_ref, kgs_ref, kgw_ref, xg_ref, s8_ref, s8t_ref, s2_ref, s2t_ref,
                    q_ref, kvc_ref, kvs_ref, kvsb_ref, kvw_ref, kvwb_ref, gate_ref, u_ref, qx_ref):
    hb = _rms(x_ref[...], nm_ref[...]).astype(BF16)
    y = _dot(hb, w_ref[...])
    q_ref[...] = _segnorm64(y[:, C_Q:C_KVC], s8_ref[...], s8t_ref[...], qg_ref[...])
    kvc_ref[...] = y[:, C_KVC:C_KVS]
    for c0, gain_ref, o_ref, ob_ref in ((C_KVS, kgs_ref, kvs_ref, kvsb_ref), (C_KVW, kgw_ref, kvw_ref, kvwb_ref)):
        kn = _segnorm64(y[:, c0:c0 + 128], s2_ref[...], s2t_ref[...], gain_ref[...])
        v = y[:, c0 + 128:c0 + 256]
        o_ref[:, 0:128] = kn
        o_ref[:, 128:256] = v
        ob_ref[:, 0:128] = kn.astype(BF16)
        ob_ref[:, 128:256] = v.astype(BF16)
    gate_ref[...] = jax.nn.sigmoid(y[:, C_GATE:C_END])
    u_ref[...] = y[:, C_GLU_A:C_GLU_B] * jax.nn.sigmoid(y[:, C_GLU_B:C_QX])
    xg = xg_ref[...]
    for h in range(X_HEADS):
        seg = y[:, C_QX + 128 * h:C_QX + 128 * (h + 1)]
        qx_ref[:, 128 * h:128 * (h + 1)] = _rms(seg, xg)


def _proj_in(x, nm, w, qg, kgs, kgw, xg):
    M = x.shape[0]
    tm = min(TM_PROJ, M)
    s8, s8t = _seg_mats(512)
    s2, s2t = _seg_mats(128)
    row = lambda i: (i, 0)
    fix = lambda i: (0, 0)
    widths = [(512, F32), (256, F32), (256, F32), (256, BF16), (256, F32), (256, BF16), (128, F32), (512, F32), (512, F32)]
    return pl.pallas_call(
        _proj_in_kernel,
        grid=(M // tm,),
        in_specs=[pl.BlockSpec((tm, D_MODEL), row), pl.BlockSpec((1, D_MODEL), fix),
                  pl.BlockSpec((D_MODEL, C_END), fix), pl.BlockSpec((1, 512), fix), pl.BlockSpec((1, 128), fix),
                  pl.BlockSpec((1, 128), fix), pl.BlockSpec((1, 128), fix),
                  pl.BlockSpec((512, LANES), fix), pl.BlockSpec((LANES, 512), fix),
                  pl.BlockSpec((128, LANES), fix), pl.BlockSpec((LANES, 128), fix)],
        out_specs=[pl.BlockSpec((tm, wd), row) for wd, _ in widths],
        out_shape=[jax.ShapeDtypeStruct((M, wd), dt) for wd, dt in widths],
        compiler_params=_cparams("parallel"),
        name="proj_in",
    )(x, nm, w, qg, kgs, kgw, xg, s8, s8t, s2, s2t)


def _cmp_matmul_kernel(pt_ref, *refs):
    del pt_ref
    page_refs, w_ref, o_ref = refs[:-2], refs[-2], refs[-1]
    rows = jnp.concatenate([r[...] for r in page_refs], axis=0).astype(BF16)
    o_ref[...] = _dot(rows, w_ref[...])


def _cmp_matmul(pages, page_ids, w1big):
    n = page_ids.shape[0]
    P = CMP_PAGES
    page_spec = lambda k: pl.BlockSpec((None, 8, 4096), lambda i, pt: (pt[i * P + k], 0, 0))
    return pl.pallas_call(
        _cmp_matmul_kernel,
        grid_spec=pltpu.PrefetchScalarGridSpec(
            num_scalar_prefetch=1,
            grid=(n // P,),
            in_specs=[page_spec(k) for k in range(P)] + [pl.BlockSpec((4096, 1024), lambda i, pt: (0, 0))],
            out_specs=pl.BlockSpec((P * 8, 1024), lambda i, pt: (i, 0)),
        ),
        out_shape=jax.ShapeDtypeStruct((n * 8, 1024), F32),
        compiler_params=_cparams("arbitrary"),
        name="cmp_matmul",
    )(page_ids, *([pages] * P), w1big)


def _gelu_tanh(x):
    return 0.5 * x * (1.0 + jnp.tanh(np.float32(np.sqrt(2.0 / np.pi)) * (x + 0.044715 * (x * x * x))))


def _cmp_finish_kernel(fs_ref, pos_ref, w1f_ref, w1fl_ref, w2_ref, kg_ref, s2_ref, s2t_ref, kc_ref, vc_ref):
    nc = fs_ref.shape[0]
    ph, plo = _split(pos_ref[...])
    pb = _dot(ph, w1f_ref[...]) + _dot(plo, w1f_ref[...]) + _dot(ph, w1fl_ref[...])
    lane = _iota((1, 512), 1)
    pos_bias = jnp.where(lane < 256, jnp.tile(pb[0:1], (1, 4)), jnp.tile(pb[1:2], (1, 4)))
    first = fs_ref[:, 0:512]
    second = pltpu.roll(fs_ref[:, 512:1024], nc - 1, 0)
    hid = _gelu_tanh(first + second + pos_bias)
    out = _dot(hid.astype(BF16), w2_ref[...])
    kc_ref[...] = _segnorm64(out[:, 0:128], s2_ref[...], s2t_ref[...], kg_ref[...])
    vc_ref[...] = out[:, 128:256]


def _cmp_finish(fs, nb, pos2, w1flat, w1flat_lo, w2big, kg):
    nc = fs.shape[0] // nb
    s2, s2t = _seg_mats(128)
    fix = lambda b: (0, 0)
    return pl.pallas_call(
        _cmp_finish_kernel,
        grid=(nb,),
        in_specs=[pl.BlockSpec((nc, 1024), lambda b: (b, 0)), pl.BlockSpec((8, 4096), fix),
                  pl.BlockSpec((4096, 128), fix), pl.BlockSpec((4096, 128), fix), pl.BlockSpec((512, 256), fix),
                  pl.BlockSpec((1, 128), fix), pl.BlockSpec((128, LANES), fix), pl.BlockSpec((LANES, 128), fix)],
        out_specs=[pl.BlockSpec((None, nc, 128), lambda b: (b, 0, 0))] * 2,
        out_shape=[jax.ShapeDtypeStruct((nb, nc, 128), F32)] * 2,
        compiler_params=_cparams("parallel"),
        name="cmp_finish",
    )(fs, pos2, w1flat, w1flat_lo, w2big, kg, s2, s2t)


def _qpad_group(q, g):
    lane = _iota((q.shape[0], LANES), 1)
    keep = (lane < HEAD_DIM) if g == 0 else (lane >= HEAD_DIM)
    parts = []
    for r in range(Q_PER_KV):
        h = Q_PER_KV * g + r
        x = q[:, 128 * (h // 2):128 * (h // 2 + 1)]
        if h % 2 != g:
            x = pltpu.roll(x, HEAD_DIM, 1)
        parts.append(jnp.where(keep, x * (HEAD_DIM ** -0.5), 0.0))
    return jnp.concatenate(parts, axis=0).astype(BF16)


def _slopes(rows_per_head, g):
    r = _iota((Q_PER_KV * rows_per_head, 1), 0) >> (rows_per_head.bit_length() - 1)
    out = jnp.zeros(r.shape, F32)
    for k in range(Q_PER_KV):
        out = jnp.where(r == k, np.float32(2.0 ** (-(Q_PER_KV * g + k + 1))), out)
    return out


def _tile4(x):
    return jnp.concatenate([x] * Q_PER_KV, axis=0)


def _top_select(score, n_top):
    lane = _iota(score.shape, 1).astype(F32)
    big = np.float32(score.shape[1])
    sel = jnp.zeros(score.shape, F32)
    sc = score
    for _ in range(n_top):
        m = jnp.max(sc, axis=-1, keepdims=True)
        idx = jnp.min(jnp.where(sc == m, lane, big), axis=-1, keepdims=True)
        pick = lane == idx
        sel = jnp.where(pick, 1.0, sel)
        sc = jnp.where(pick, -3.0e38, sc)
    return sel


def _select_blocks(p, cover, qpos, n_blk_real):
    R = qpos.shape[0]
    psum = p[0:R] + p[R:2 * R] + p[2 * R:3 * R] + p[3 * R:4 * R]
    imp = _dot_hl(psum, cover)
    blk = _iota(imp.shape, 1)
    cur = qpos >> 6
    valid = (blk * L_SEL <= qpos) & (blk < n_blk_real)
    forced = (blk == 0) | (blk == cur) | (blk == cur - 1)
    score = jnp.where(valid, imp + jnp.where(forced, FORCE_BONUS, 0.0), -FORCE_BONUS)
    score = jnp.where(blk < n_blk_real, score, -2.0e38)
    sel = _top_select(score, N_SEL_TOP)
    return jnp.where(valid, sel, 0.0)


def _merge_heads(gates, oc, os_, ow, R):
    lane = _iota((R, LANES), 1)
    tiles = []
    for t in range(N_HEADS // 2):
        g = t // 2
        halves = []
        for h in (2 * t, 2 * t + 1):
            r = h % Q_PER_KV
            sl = slice(r * R, (r + 1) * R)
            o = (gates[:, 3 * h:3 * h + 1] * oc[g][sl] + gates[:, 3 * h + 1:3 * h + 2] * os_[g][sl]
                 + gates[:, 3 * h + 2:3 * h + 3] * ow[g][sl])
            halves.append(o)
        lo, hi = halves
        if g == 0:
            hi = pltpu.roll(hi, HEAD_DIM, 1)
        else:
            lo = pltpu.roll(lo, HEAD_DIM, 1)
        tiles.append(jnp.where(lane < HEAD_DIM, lo, hi))
    return jnp.concatenate(tiles, axis=1)


def _nsa_prompt_kernel(q_ref, g_ref, kc_ref, vc_ref, kvs_ref, kvw_ref, cov_ref, e_ref, o_ref):
    R = Q_BLOCK
    i = pl.program_id(1)
    start = i * R
    q = q_ref[...]
    qpos = start + _iota((R, 1), 0)
    qpos4 = _tile4(qpos)
    n_cmp = kc_ref.shape[0]
    n_chunks = (start + R + SEL_CHUNK - 1) // SEL_CHUNK
    ws = pl.multiple_of(jnp.maximum(start - WINDOW, 0), R)
    oc, os_, ow = [], [], []
    for g in range(N_KV):
        qg = _qpad_group(q, g)
        slope = _slopes(R, g)
        d = qpos4 - (_iota((1, n_cmp), 1) * S_CMP + (L_CMP - 1))
        s = _dot_nt(qg, kc_ref[...].astype(BF16)) - slope * d.astype(F32)
        p = _masked_softmax(s, d >= 0)
        oc.append(_dot(p.astype(BF16), vc_ref[...].astype(BF16)))
        selm = _select_blocks(p, cov_ref[...], qpos, e_ref.shape[1]).astype(BF16)

        def body(c, carry):
            m, l, acc = carry
            k0 = pl.multiple_of(c * SEL_CHUNK, SEL_CHUNK)
            kv = kvs_ref[pl.ds(k0, SEL_CHUNK), :]
            dd = qpos4 - (k0 + _iota((1, SEL_CHUNK), 1))
            mask = (_tile4(_dot(selm, e_ref[c])) > 0.5) & (dd >= 0)
            sc = jnp.where(mask, _dot_nt(qg, kv[:, 0:128]) - slope * dd.astype(F32), NEG)
            m_new = jnp.maximum(m, jnp.max(sc, axis=-1, keepdims=True))
            a = jnp.exp(m - m_new)
            e = jnp.where(mask, jnp.exp(sc - m_new), 0.0)
            return (m_new, a * l + jnp.sum(e, axis=-1, keepdims=True),
                    a * acc + _dot(e.astype(BF16), kv[:, 128:256]))

        init = (jnp.full((4 * R, 1), NEG, F32), jnp.zeros((4 * R, 1), F32), jnp.zeros((4 * R, LANES), F32))
        _, l, acc = lax.fori_loop(0, n_chunks, body, init)
        os_.append(acc / jnp.maximum(l, TINY))
        kv = kvw_ref[pl.ds(ws, WINDOW + R), :]
        d = qpos4 - (ws + _iota((1, WINDOW + R), 1))
        s = _dot_nt(qg, kv[:, 0:128]) - slope * d.astype(F32)
        p = _masked_softmax(s, (d >= 0) & (d < WINDOW))
        ow.append(_dot(p.astype(BF16), kv[:, 128:256]))
    o_ref[...] = _merge_heads(g_ref[...], oc, os_, ow, R)


def _nsa_prompt(q, gates, kc, vc, kvs_b, kvw_b):
    B, T, _ = q.shape
    n_cmp = kc.shape[1]
    n_sel = T // L_SEL
    n_ch = T // SEL_CHUNK
    start = np.arange(n_cmp)[:, None] * S_CMP
    j = np.arange(n_sel)[None, :]
    cover = ((start < (j + 1) * L_SEL) & (start + L_CMP > j * L_SEL) & (np.arange(n_cmp)[:, None] < T // S_CMP - 1))
    key_blk = (np.arange(n_ch)[:, None, None] * SEL_CHUNK + np.arange(SEL_CHUNK)[None, None, :]) // L_SEL
    expand = key_blk == np.arange(n_sel)[None, :, None]
    blk = lambda b, i: (b, i, 0)
    per_b = lambda b, i: (b, 0, 0)
    return pl.pallas_call(
        _nsa_prompt_kernel,
        grid=(B, T // Q_BLOCK),
        in_specs=[pl.BlockSpec((None, Q_BLOCK, 512), blk), pl.BlockSpec((None, Q_BLOCK, 128), blk),
                  pl.BlockSpec((None, n_cmp, 128), per_b), pl.BlockSpec((None, n_cmp, 128), per_b),
                  pl.BlockSpec((None, T, 256), per_b), pl.BlockSpec((None, T, 256), per_b),
                  pl.BlockSpec((n_cmp, n_sel), lambda b, i: (0, 0)),
                  pl.BlockSpec((n_ch, n_sel, SEL_CHUNK), lambda b, i: (0, 0, 0))],
        out_specs=pl.BlockSpec((None, Q_BLOCK, 512), blk),
        out_shape=jax.ShapeDtypeStruct((B, T, 512), F32),
        compiler_params=_cparams("parallel", "arbitrary"),
        name="nsa_prompt",
    )(q, gates, kc, vc, kvs_b, kvw_b, jnp.asarray(cover, BF16), jnp.asarray(expand, BF16))


def _nsa_sample_kernel(pt_ref, q_ref, g_ref, kc_ref, vc_ref, cov_ref, kvn_ref, win_ref, kwn_ref, *rest,
                       past, n_blk_real):
    del pt_ref
    P = SAMPLE_PAGES
    page_refs = rest[:P]
    o_ref = rest[P]
    qg_s, selm_s, m_s, l_s, acc_s, oc_s, ow_s = rest[P + 1:]
    R = T_SAMPLE_PAD
    j = pl.program_id(1)
    nj = pl.num_programs(1)
    qpos = past + _iota((R, 1), 0)
    qpos4 = _tile4(qpos)
    n_cmp = kc_ref.shape[0]
    nb = cov_ref.shape[1]
    chunk = P * PAGE_SIZE

    @pl.when(j == 0)
    def _():
        q = q_ref[...]
        for g in range(N_KV):
            qg = _qpad_group(q, g)
            qg_s[g] = qg
            slope = _slopes(R, g)
            d = qpos4 - (_iota((1, n_cmp), 1) * S_CMP + (L_CMP - 1))
            s = _dot_nt(qg, kc_ref[...].astype(BF16)) - slope * d.astype(F32)
            p = _masked_softmax(s, (d >= 0) & (_iota((1, n_cmp), 1) < n_cmp - 1))
            oc_s[g] = _dot(p.astype(BF16), vc_ref[...].astype(BF16))
            selm_s[g] = _select_blocks(p, cov_ref[...], qpos, n_blk_real)
            zpad = jnp.zeros((LANES - R, 256), F32)
            kv = jnp.concatenate([win_ref[...], kwn_ref[...], zpad], axis=0).astype(BF16)
            col = _iota((1, WINDOW + LANES), 1)
            d = qpos4 - (past - WINDOW + col)
            s = _dot_nt(qg, kv[:, 0:128]) - slope * d.astype(F32)
            p = _masked_softmax(s, (d >= 0) & (d < WINDOW) & (col < WINDOW + 4))
            ow_s[g] = _dot(p.astype(BF16), kv[:, 128:256])
            m_s[g] = jnp.full((4 * R, 1), NEG, F32)
            l_s[g] = jnp.zeros((4 * R, 1), F32)
            acc_s[g] = jnp.zeros((4 * R, LANES), F32)

    def update(g, kv, kpos, extra_mask):
        slope = _slopes(R, g)
        nk = kv.shape[0]
        expand = jnp.where(_iota((nb, nk), 0) == (kpos >> 6), 1.0, 0.0).astype(BF16)
        dd = qpos4 - kpos
        mask = (_tile4(_dot(selm_s[g].astype(BF16), expand)) > 0.5) & (dd >= 0)
        if extra_mask is not None:
            mask = mask & extra_mask
        sc = jnp.where(mask, _dot_nt(qg_s[g], kv[:, 0:128]) - slope * dd.astype(F32), NEG)
        m = m_s[g]
        m_new = jnp.maximum(m, jnp.max(sc, axis=-1, keepdims=True))
        a = jnp.exp(m - m_new)
        e = jnp.where(mask, jnp.exp(sc - m_new), 0.0)
        m_s[g] = m_new
        l_s[g] = a * l_s[g] + jnp.sum(e, axis=-1, keepdims=True)
        acc_s[g] = a * acc_s[g] + _dot(e.astype(BF16), kv[:, 128:256])

    kv_pages = jnp.concatenate([r[...] for r in page_refs], axis=0).astype(BF16)
    kpos_pages = j * chunk + _iota((1, chunk), 1)
    for g in range(N_KV):
        update(g, kv_pages, kpos_pages, None)

    @pl.when(j == nj - 1)
    def _():
        zpad = jnp.zeros((LANES - R, 256), F32)
        kv_new = jnp.concatenate([kvn_ref[...], zpad], axis=0).astype(BF16)
        col = _iota((1, LANES), 1)
        for g in range(N_KV):
            update(g, kv_new, past + col, col < 4)
        os_ = [acc_s[g] / jnp.maximum(l_s[g], TINY) for g in range(N_KV)]
        o_ref[...] = _merge_heads(g_ref[...], [oc_s[0], oc_s[1]], os_, [ow_s[0], ow_s[1]], R)


def _nsa_sample(q, gates, kc, vc, kvs_new, win, kvw_new, sel_pages, page_table):
    DB = q.shape[0]
    n_pages = page_table.shape[1]
    past = n_pages * PAGE_SIZE
    P = SAMPLE_PAGES
    R = T_SAMPLE_PAD
    n_cmp = kc.shape[1]
    n_blk_real = -(-(past + 4) // L_SEL)
    nb = -(-n_blk_real // LANES) * LANES
    start = np.arange(n_cmp)[:, None] * S_CMP
    jj = np.arange(nb)[None, :]
    cover = ((start < (jj + 1) * L_SEL) & (start + L_CMP > jj * L_SEL) & (jj < n_blk_real)
             & (np.arange(n_cmp)[:, None] < n_cmp - 1))
    per_b = lambda b, j, pt: (b, 0, 0)
    page_spec = lambda k: pl.BlockSpec((None, PAGE_SIZE, 256), lambda b, j, pt: (pt[b * n_pages + j * P + k], 0, 0))
    kern = functools.partial(_nsa_sample_kernel, past=past, n_blk_real=n_blk_real)
    return pl.pallas_call(
        kern,
        grid_spec=pltpu.PrefetchScalarGridSpec(
            num_scalar_prefetch=1,
            grid=(DB, n_pages // P),
            in_specs=[pl.BlockSpec((None, R, 512), per_b), pl.BlockSpec((None, R, 128), per_b),
                      pl.BlockSpec((None, n_cmp, 128), per_b), pl.BlockSpec((None, n_cmp, 128), per_b),
                      pl.BlockSpec((n_cmp, nb), lambda b, j, pt: (0, 0)),
                      pl.BlockSpec((None, R, 256), per_b), pl.BlockSpec((None, WINDOW, 256), per_b),
                      pl.BlockSpec((None, R, 256), per_b)] + [page_spec(k) for k in range(P)],
            out_specs=pl.BlockSpec((None, R, 512), per_b),
            scratch_shapes=[pltpu.VMEM((N_KV, 4 * R, LANES), BF16), pltpu.VMEM((N_KV, R, nb), F32),
                            pltpu.VMEM((N_KV, 4 * R, 1), F32), pltpu.VMEM((N_KV, 4 * R, 1), F32),
                            pltpu.VMEM((N_KV, 4 * R, LANES), F32), pltpu.VMEM((N_KV, 4 * R, LANES), F32),
                            pltpu.VMEM((N_KV, 4 * R, LANES), F32)],
        ),
        out_shape=jax.ShapeDtypeStruct((DB, R, 512), F32),
        compiler_params=_cparams("parallel", "arbitrary"),
        name="nsa_sample",
    )(page_table.reshape(-1), q, gates, kc, vc, jnp.asarray(cover, BF16), kvs_new, win, kvw_new,
      *([sel_pages] * P))


def _ln_silu(c, g, b):
    mu = jnp.mean(c, axis=-1, keepdims=True)
    xc = c - mu
    y = xc * lax.rsqrt(jnp.mean(xc * xc, axis=-1, keepdims=True) + EPS) * g + b
    return y * jax.nn.sigmoid(y)


def _conv_prompt_kernel(um_ref, up_ref, w_ref, b_ref, g_ref, bb_ref, o_ref, ext_ref):
    tt = um_ref.shape[0]
    i = pl.program_id(1)
    ext_ref[0:32, :] = jnp.where(i > 0, up_ref[...], 0.0)
    ext_ref[32:32 + tt, :] = um_ref[...]
    acc = jnp.zeros((tt, CONV_CH), F32)
    for j in range(CONV_W):
        acc = acc + ext_ref[pl.ds(j + 2, tt), :] * w_ref[j:j + 1, :]
    o_ref[...] = _ln_silu(acc + b_ref[...], g_ref[...], bb_ref[...])


def _conv_prompt(u, w, b, g, bb):
    B, T, C = u.shape
    tt = 512
    fix = lambda bi, i: (0, 0)
    return pl.pallas_call(
        _conv_prompt_kernel,
        grid=(B, T // tt),
        in_specs=[pl.BlockSpec((None, tt, C), lambda bi, i: (bi, i, 0)),
                  pl.BlockSpec((None, 32, C), lambda bi, i: (bi, jnp.maximum(i * (tt // 32) - 1, 0), 0)),
                  pl.BlockSpec((32, C), fix), pl.BlockSpec((1, C), fix), pl.BlockSpec((1, C), fix),
                  pl.BlockSpec((1, C), fix)],
        out_specs=pl.BlockSpec((None, tt, C), lambda bi, i: (bi, i, 0)),
        out_shape=jax.ShapeDtypeStruct((B, T, C), F32),
        scratch_shapes=[pltpu.VMEM((tt + 32, C), F32)],
        compiler_params=_cparams("parallel", "arbitrary"),
        name="conv_prompt",
    )(u, u, w, b, g, bb)


def _conv_sample_kernel(ext_ref, w_ref, b_ref, g_ref, bb_ref, o_ref):
    n_out = o_ref.shape[0]
    for t in range(n_out):
        acc = jnp.zeros(o_ref.shape[1:], F32)
        for j in range(CONV_W):
            acc = acc + ext_ref[t + j] * w_ref[j:j + 1, :]
        o_ref[t] = _ln_silu(acc + b_ref[...], g_ref[...], bb_ref[...])


def _conv_sample(ext_t, w, b, g, bb, n_out):
    _, DB, C = ext_t.shape
    return pl.pallas_call(
        _conv_sample_kernel,
        out_shape=jax.ShapeDtypeStruct((n_out, DB, C), F32),
        name="conv_sample",
    )(ext_t, w, b, g, bb)


def _mem_kv_kernel(m_ref, g_ref, w_ref, kg_ref, o_ref):
    hb = _rms(m_ref[...], g_ref[...]).astype(BF16)
    kv = _dot(hb, w_ref[...])
    kg = kg_ref[...]
    for h in range(X_HEADS):
        o_ref[:, 128 * h:128 * (h + 1)] = _rms(kv[:, 128 * h:128 * (h + 1)], kg)
    o_ref[:, 512:1024] = kv[:, 512:1024]


def _mem_kv(mem, g, w, kg):
    B, M, D = mem.shape
    fix = lambda b: (0, 0)
    return pl.pallas_call(
        _mem_kv_kernel,
        grid=(B,),
        in_specs=[pl.BlockSpec((None, M, D), lambda b: (b, 0, 0)), pl.BlockSpec((1, D), fix),
                  pl.BlockSpec((D, 1024), fix), pl.BlockSpec((1, 128), fix)],
        out_specs=pl.BlockSpec((None, M, 1024), lambda b: (b, 0, 0)),
        out_shape=jax.ShapeDtypeStruct((B, M, 1024), F32),
        compiler_params=_cparams("parallel"),
        name="mem_kv",
    )(mem, g, w, kg)


def _mem_attend_kernel(q_ref, kv_ref, o_ref):
    for h in range(X_HEADS):
        q = q_ref[:, 128 * h:128 * (h + 1)].astype(BF16)
        k = kv_ref[:, 128 * h:128 * (h + 1)].astype(BF16)
        v = kv_ref[:, 512 + 128 * h:512 + 128 * (h + 1)].astype(BF16)
        s = _dot_nt(q, k) * (X_HEAD_DIM ** -0.5)
        e = jnp.exp(s - jnp.max(s, axis=-1, keepdims=True))
        p = e / jnp.sum(e, axis=-1, keepdims=True)
        o_ref[:, 128 * h:128 * (h + 1)] = _dot(p.astype(BF16), v)


def _mem_attend(qx, mkv, tq):
    B, T, _ = qx.shape
    M = mkv.shape[1]
    return pl.pallas_call(
        _mem_attend_kernel,
        grid=(B, T // tq),
        in_specs=[pl.BlockSpec((None, tq, 512), lambda b, i: (b, i, 0)),
                  pl.BlockSpec((None, M, 1024), lambda b, i: (b, 0, 0))],
        out_specs=pl.BlockSpec((None, tq, 512), lambda b, i: (b, i, 0)),
        out_shape=jax.ShapeDtypeStruct((B, T, 512), F32),
        compiler_params=_cparams("parallel", "arbitrary"),
        name="mem_attend",
    )(qx, mkv)


def _tail_kernel(x_ref, on_ref, ocv_ref, om_ref, nm_ref, wb_ref, wg_ref, bg_ref, wo_ref, nf_ref,
                 wrh_ref, wrl_ref, br_ref, x1_ref, xn_ref, route_ref):
    x = x_ref[...]
    hb = _rms(x, nm_ref[...]).astype(BF16)
    gate = jax.nn.sigmoid(_dot(hb, wg_ref[...]) + bg_ref[...])
    y_nsa = _dot(on_ref[...].astype(BF16), wb_ref[0:512, :])
    y_conv = _dot(ocv_ref[...].astype(BF16), wb_ref[512:1024, :])
    y_mem = _dot(om_ref[...].astype(BF16), wb_ref[1024:1536, :])
    merged = gate[:, 0:1024] * y_nsa + gate[:, 1024:2048] * y_conv + gate[:, 2048:3072] * y_mem
    x1 = x + _dot(merged.astype(BF16), wo_ref[...])
    x1_ref[...] = x1
    xn = _rms(x1, nf_ref[...])
    xn_ref[...] = xn
    xh, xl = _split(xn)
    logits = _dot(xh, wrh_ref[...]) + _dot(xl, wrh_ref[...]) + _dot(xh, wrl_ref[...]) + br_ref[...]
    lane = _iota(logits.shape, 1)
    lanef = lane.astype(F32)
    is_g = lane < N_GROUPS
    gl = jnp.where(is_g, logits, NEG)
    gm = jnp.max(gl, axis=-1, keepdims=True)
    p_g = 1.0 / jnp.sum(jnp.where(is_g, jnp.exp(gl - gm), 0.0), axis=-1, keepdims=True)
    g_sel = jnp.min(jnp.where(gl == gm, lanef, 128.0), axis=-1, keepdims=True).astype(jnp.int32)
    lo = N_GROUPS + EXP_PER_GROUP * g_sel
    el = jnp.where((lane >= lo) & (lane < lo + EXP_PER_GROUP), logits, NEG)
    m0 = jnp.max(el, axis=-1, keepdims=True)
    i0 = jnp.min(jnp.where(el == m0, lanef, 128.0), axis=-1, keepdims=True)
    el = jnp.where(lanef == i0, 2.0 * NEG, el)
    m1 = jnp.max(el, axis=-1, keepdims=True)
    i1 = jnp.min(jnp.where(el == m1, lanef, 128.0), axis=-1, keepdims=True)
    t = jnp.exp(m1 - m0)
    w0 = p_g / (1.0 + t)
    w1 = p_g * t / (1.0 + t)
    route_ref[...] = jnp.where(lane == 0, i0 - N_GROUPS, jnp.where(lane == 1, i1 - N_GROUPS,
                               jnp.where(lane == 2, w0, jnp.where(lane == 3, w1, 0.0))))


def _tail(x, o_nsa, o_conv, o_mem, nm, wb, wg, bg, wo, nf, wrh, wrl, br):
    M = x.shape[0]
    tm = min(TM_TAIL, M)
    row = lambda i: (i, 0)
    fix = lambda i: (0, 0)
    return pl.pallas_call(
        _tail_kernel,
        grid=(M // tm,),
        in_specs=[pl.BlockSpec((tm, D_MODEL), row), pl.BlockSpec((tm, 512), row), pl.BlockSpec((tm, 512), row),
                  pl.BlockSpec((tm, 512), row), pl.BlockSpec((1, D_MODEL), fix), pl.BlockSpec((1536, D_MODEL), fix),
                  pl.BlockSpec((D_MODEL, 3072), fix), pl.BlockSpec((1, 3072), fix), pl.BlockSpec((D_MODEL, D_MODEL), fix),
                  pl.BlockSpec((1, D_MODEL), fix), pl.BlockSpec((D_MODEL, LANES), fix), pl.BlockSpec((D_MODEL, LANES), fix),
                  pl.BlockSpec((1, LANES), fix)],
        out_specs=[pl.BlockSpec((tm, D_MODEL), row), pl.BlockSpec((tm, D_MODEL), row), pl.BlockSpec((tm, LANES), row)],
        out_shape=[jax.ShapeDtypeStruct((M, D_MODEL), F32), jax.ShapeDtypeStruct((M, D_MODEL), F32),
                   jax.ShapeDtypeStruct((M, LANES), F32)],
        compiler_params=_cparams("parallel"),
        name="tail",
    )(x, o_nsa, o_conv, o_mem, nm, wb, wg, bg, wo, nf, wrh, wrl, br)


def _moe_kernel(be_ref, nact_ref, src_ref, dst_ref, ws_ref, xn_hbm, wu_ref, wd_ref, y_hbm, xbuf, ybuf, gsem, ssem):
    del be_ref
    i = pl.program_id(0)

    def gather(r):
        return pltpu.make_async_copy(xn_hbm.at[pl.ds(src_ref[0, 0, r], 1)], xbuf.at[pl.ds(r, 1)], gsem)

    def scatter(r):
        return pltpu.make_async_copy(ybuf.at[pl.ds(r, 1)], y_hbm.at[pl.ds(dst_ref[0, 0, r], 1)], ssem)

    @pl.when(i < nact_ref[0])
    def _():
        def g_start(r, c):
            gather(r).start()
            return c

        def g_wait(r, c):
            gather(r).wait()
            return c

        lax.fori_loop(0, MOE_BLOCK, g_start, 0)
        lax.fori_loop(0, MOE_BLOCK, g_wait, 0)
        ab = _dot(xbuf[...].astype(BF16), wu_ref[...])
        a, b = ab[:, 0:D_EXPERT], ab[:, D_EXPERT:2 * D_EXPERT]
        hmid = (a * jax.nn.sigmoid(a) * b).astype(BF16)
        ybuf[...] = _dot(hmid, wd_ref[...]) * ws_ref[...]

        def s_start(r, c):
            @pl.when(dst_ref[0, 0, r] >= 0)
            def _():
                scatter(r).start()
            return c

        def s_wait(r, c):
            @pl.when(dst_ref[0, 0, r] >= 0)
            def _():
                scatter(r).wait()
            return c

        lax.fori_loop(0, MOE_BLOCK, s_start, 0)
        lax.fori_loop(0, MOE_BLOCK, s_wait, 0)


def _moe(xn, route, w_up, w_down):
    N, D = xn.shape
    M = N * TOP_IN_GROUP
    e_id = route[:, 0:2].astype(jnp.int32).reshape(-1)
    e_w = route[:, 2:4].reshape(-1)
    order = jnp.argsort(e_id)
    e_s = e_id[order]
    counts = jnp.bincount(e_id, length=N_EXPERTS)
    padded = (counts + MOE_BLOCK - 1) // MOE_BLOCK * MOE_BLOCK
    start = jnp.cumsum(counts) - counts
    pend = jnp.cumsum(padded)
    dest = (pend - padded)[e_s] + jnp.arange(M) - start[e_s]
    n_blk = -(-M // MOE_BLOCK) + N_EXPERTS
    rows = n_blk * MOE_BLOCK
    src = jnp.zeros((rows,), jnp.int32).at[dest].set((order // TOP_IN_GROUP).astype(jnp.int32))
    dst = jnp.full((rows,), -1, jnp.int32).at[dest].set(((order % TOP_IN_GROUP) * N + order // TOP_IN_GROUP).astype(jnp.int32))
    ws = jnp.zeros((rows,), F32).at[dest].set(e_w[order])
    blk_e = jnp.clip(jnp.searchsorted(pend, jnp.arange(n_blk) * MOE_BLOCK, side='right'), 0, N_EXPERTS - 1).astype(jnp.int32)
    n_act = (pend[-1] // MOE_BLOCK).astype(jnp.int32).reshape(1)
    idx_spec = pl.BlockSpec((1, 1, MOE_BLOCK), lambda i, be, na: (i, 0, 0), memory_space=pltpu.SMEM)
    return pl.pallas_call(
        _moe_kernel,
        grid_spec=pltpu.PrefetchScalarGridSpec(
            num_scalar_prefetch=2,
            grid=(n_blk,),
            in_specs=[idx_spec, idx_spec,
                      pl.BlockSpec((None, MOE_BLOCK, 1), lambda i, be, na: (i, 0, 0)),
                      pl.BlockSpec(memory_space=pl.ANY),
                      pl.BlockSpec((None, D, 2 * D_EXPERT), lambda i, be, na: (be[i], 0, 0)),
                      pl.BlockSpec((None, D_EXPERT, D), lambda i, be, na: (be[i], 0, 0))],
            out_specs=pl.BlockSpec(memory_space=pl.ANY),
            scratch_shapes=[pltpu.VMEM((MOE_BLOCK, D), F32), pltpu.VMEM((MOE_BLOCK, D), F32),
                            pltpu.SemaphoreType.DMA(()), pltpu.SemaphoreType.DMA(())],
        ),
        out_shape=jax.ShapeDtypeStruct((TOP_IN_GROUP * N, D), F32),
        compiler_params=_cparams("arbitrary"),
        name="moe",
    )(blk_e, n_act, src.reshape(n_blk, 1, MOE_BLOCK), dst.reshape(n_blk, 1, MOE_BLOCK),
      ws.reshape(n_blk, MOE_BLOCK, 1), xn, w_up, w_down)


def _combine_kernel(x_ref, y0_ref, y1_ref, o_ref):
    o_ref[...] = x_ref[...] + (y0_ref[...] + y1_ref[...])


def _combine(x1, y2):
    N, D = x1.shape
    tm = min(512, N)
    nb = N // tm
    return pl.pallas_call(
        _combine_kernel,
        grid=(nb,),
        in_specs=[pl.BlockSpec((tm, D), lambda i: (i, 0)), pl.BlockSpec((tm, D), lambda i: (i, 0)),
                  pl.BlockSpec((tm, D), lambda i: (i + nb, 0))],
        out_specs=pl.BlockSpec((tm, D), lambda i: (i, 0)),
        out_shape=jax.ShapeDtypeStruct((N, D), F32),
        compiler_params=_cparams("parallel"),
        name="combine",
    )(x1, y2, y2)


def _layer_weights(l, p):
    w = p['w_in'][l]
    n_gate = 3 * N_HEADS
    w_re = jnp.concatenate([w[:, :1280], w[:, 1280 + n_gate:], w[:, 1280:1280 + n_gate],
                            jnp.zeros((D_MODEL, C_END - C_GATE - n_gate), F32)], axis=1).astype(BF16)
    w1 = p['w_cmp1'][l]
    eye = jnp.eye(2, dtype=F32)
    halves = []
    for half in range(2):
        wh = w1[:, S_CMP * half:S_CMP * (half + 1)]
        big = jnp.einsum('kpdf,kK,gG->pkgdKGf', wh, eye, eye)
        halves.append(big.reshape(S_CMP * 256, 4 * CMP_HIDDEN))
    w1big = jnp.concatenate(halves, axis=1).astype(BF16)
    w1flat = jnp.concatenate([w1[0].reshape(L_CMP * HEAD_DIM, CMP_HIDDEN), w1[1].reshape(L_CMP * HEAD_DIM, CMP_HIDDEN)], axis=0)
    pos = p['pos_cmp'][l].reshape(2, L_CMP * HEAD_DIM)
    pos2 = jnp.zeros((8, 4096), F32).at[0, :2048].set(pos[0]).at[1, 2048:].set(pos[1])
    w1f_hi = w1flat.astype(BF16)
    w1f_lo = (w1flat - w1f_hi.astype(F32)).astype(BF16)
    w2 = p['w_cmp2'][l]
    w2big = jnp.einsum('kfd,kK,gG->kgfKGd', w2, eye, eye).reshape(4 * CMP_HIDDEN, 256).astype(BF16)
    wr = jnp.concatenate([p['w_grp'][l], p['w_route'][l], jnp.zeros((D_MODEL, LANES - N_GROUPS - N_EXPERTS), F32)], axis=1)
    wr_hi = wr.astype(BF16)
    wr_lo = (wr - wr_hi.astype(F32)).astype(BF16)
    br = jnp.concatenate([p['b_grp'][l], p['b_route'][l], jnp.zeros((LANES - N_GROUPS - N_EXPERTS,), F32)]).reshape(1, LANES)
    cw = jnp.concatenate([p['conv_w'][l], jnp.zeros((1, CONV_CH), F32)], axis=0)
    r1 = lambda a: a.reshape(1, -1)
    return dict(
        nm=r1(p['norm_mix'][l]), w_in=w_re, qg=r1(jnp.tile(p['q_gain'][l], N_HEADS)),
        kg_cmp=r1(jnp.tile(p['k_gain'][l, 0], 2)), kg_sel=r1(jnp.tile(p['k_gain'][l, 1], 2)),
        kg_win=r1(jnp.tile(p['k_gain'][l, 2], 2)), xq_gain=r1(p['mem_qk_gain'][l, 0]), xk_gain=r1(p['mem_qk_gain'][l, 1]),
        w1big=w1big, pos2=pos2, w1f_hi=w1f_hi, w1f_lo=w1f_lo, w2big=w2big,
        conv_w=cw, conv_b=r1(p['conv_b'][l]), ln_g=r1(p['conv_ln_g'][l]), ln_b=r1(p['conv_ln_b'][l]),
        mem_norm=r1(p['mem_norm'][l]), w_mem=p['w_mem_kv'][l].astype(BF16),
        wb=p['w_branch'][l].astype(BF16), wg=p['w_gate'][l].astype(BF16), bg=r1(p['b_gate'][l]),
        wo=p['w_out'][l].astype(BF16), nf=r1(p['norm_ffn'][l]), wr_hi=wr_hi, wr_lo=wr_lo, br=br,
        w_up=p['w_up'][l].astype(BF16), w_down=p['w_down'][l].astype(BF16),
    )


def _ffn(x1, xn, route, w):
    return _combine(x1, _moe(xn, route, w['w_up'], w['w_down']))


def kernel(x_prompt, x_sample, cache_cmp_kv, cache_sel_kv, state_win_kv, state_conv, cache_mem_kv, page_table, mem_prompt, norm_mix, norm_ffn, w_in, q_gain, k_gain, w_cmp1, w_cmp2, pos_cmp, conv_w, conv_b, conv_ln_g, conv_ln_b, mem_norm, w_mem_kv, mem_qk_gain, w_branch, w_gate, b_gate, w_out, w_grp, b_grp, w_route, b_route, w_up, w_down):
    params = dict(norm_mix=norm_mix, norm_ffn=norm_ffn, w_in=w_in, q_gain=q_gain, k_gain=k_gain, w_cmp1=w_cmp1,
                  w_cmp2=w_cmp2, pos_cmp=pos_cmp, conv_w=conv_w, conv_b=conv_b, conv_ln_g=conv_ln_g,
                  conv_ln_b=conv_ln_b, mem_norm=mem_norm, w_mem_kv=w_mem_kv, mem_qk_gain=mem_qk_gain,
                  w_branch=w_branch, w_gate=w_gate, b_gate=b_gate, w_out=w_out, w_grp=w_grp, b_grp=b_grp,
                  w_route=w_route, b_route=b_route, w_up=w_up, w_down=w_down)
    B, T, D = x_prompt.shape
    DB, DS, _ = x_sample.shape
    depth = w_in.shape[0]
    n_phys = cache_cmp_kv.shape[1]
    R = T_SAMPLE_PAD
    xp = x_prompt.reshape(B * T, D)
    xs = jnp.pad(x_sample, ((0, 0), (0, R - DS), (0, 0))).reshape(DB * R, D)
    kv5 = lambda a, nb, t: a.reshape(nb, t, 2, N_KV, HEAD_DIM)
    outs = [[] for _ in range(9)]
    for l in range(depth):
        w = _layer_weights(l, params)
        proj = lambda x: _proj_in(x, w['nm'], w['w_in'], w['qg'], w['kg_sel'], w['kg_win'], w['xq_gain'])
        tail = lambda x, a, c, m: _tail(x, a, c, m, w['nm'], w['wb'], w['wg'], w['bg'], w['wo'], w['nf'],
                                        w['wr_hi'], w['wr_lo'], w['br'])
        finish = lambda fs, nb: _cmp_finish(fs, nb, w['pos2'], w['w1f_hi'], w['w1f_lo'], w['w2big'], w['kg_cmp'])
        q, kvc, kvs, kvs_b, kvw, kvw_b, gates, u, qx = proj(xp)
        n_pg = B * T // PAGE_SIZE
        fs = _cmp_matmul(kvc.reshape(n_pg, 8, 4096), jnp.arange(n_pg, dtype=jnp.int32), w['w1big'])
        kc, vc = finish(fs, B)
        r3 = lambda a: a.reshape(B, T, a.shape[-1])
        o_nsa = _nsa_prompt(r3(q), r3(gates), kc, vc, r3(kvs_b), r3(kvw_b))
        o_conv = _conv_prompt(r3(u), w['conv_w'], w['conv_b'], w['ln_g'], w['ln_b'])
        m_kv = _mem_kv(mem_prompt, w['mem_norm'], w['w_mem'], w['xk_gain'])
        o_mem = _mem_attend(r3(qx), m_kv, 512)
        x1, xn, route = tail(xp, o_nsa.reshape(B * T, 512), o_conv.reshape(B * T, 512), o_mem.reshape(B * T, 512))
        xp = _ffn(x1, xn, route, w)
        n_win = min(WINDOW, T)
        outs[0].append(kv5(kvc, B, T))
        outs[1].append(kv5(kvs, B, T))
        outs[2].append(kv5(kvw, B, T)[:, T - n_win:])
        outs[3].append(r3(u)[:, T - (CONV_W - 1):])
        outs[4].append(m_kv.reshape(B, -1, 2, X_HEADS, X_HEAD_DIM))
        q, kvc, kvs, _, kvw, _, gates, u, qx = proj(xs)
        s3 = lambda a: a.reshape(DB, R, a.shape[-1])
        fs = _cmp_matmul(cache_cmp_kv[l].reshape(n_phys, 8, 4096), page_table.reshape(-1), w['w1big'])
        kc, vc = finish(fs, DB)
        o_nsa = _nsa_sample(s3(q), s3(gates), kc, vc, s3(kvs), state_win_kv[l].reshape(DB, -1, 256), s3(kvw),
                            cache_sel_kv[l].reshape(n_phys, PAGE_SIZE, 256), page_table)
        u_ext = jnp.concatenate([state_conv[l], s3(u)[:, :DS]], axis=1)
        o_conv = _conv_sample(u_ext.transpose(1, 0, 2), w['conv_w'], w['conv_b'], w['ln_g'], w['ln_b'], DS)
        o_conv = jnp.pad(o_conv.transpose(1, 0, 2), ((0, 0), (0, R - DS), (0, 0))).reshape(DB * R, CONV_CH)
        o_mem = _mem_attend(s3(qx), cache_mem_kv[l].reshape(DB, -1, 1024), R)
        x1, xn, route = tail(xs, o_nsa.reshape(DB * R, 512), o_conv, o_mem.reshape(DB * R, 512))
        real = lambda a: a.reshape(DB, R, -1)[:, :DS].reshape(DB * DS, -1)
        x2 = _ffn(real(x1), real(xn), real(route), w)
        xs = jnp.pad(x2.reshape(DB, DS, D), ((0, 0), (0, R - DS), (0, 0))).reshape(DB * R, D)
        outs[5].append(kv5(kvc, DB, R)[:, :DS])
        outs[6].append(kv5(kvs, DB, R)[:, :DS])
        win_all = jnp.concatenate([state_win_kv[l], kv5(kvw, DB, R)[:, :DS]], axis=1)
        outs[7].append(win_all[:, DS:])
        outs[8].append(u_ext[:, -(CONV_W - 1):])
    y_prompt = xp.reshape(B, T, D)
    y_sample = xs.reshape(DB, R, D)[:, :DS]
    return (y_prompt, y_sample) + tuple(jnp.stack(o) for o in outs)
```

```python
import functools

import numpy as np
import jax
import jax.numpy as jnp
from jax import lax
from jax.experimental import pallas as pl
from jax.experimental.pallas import tpu as pltpu

F32 = jnp.float32
BF16 = jnp.bfloat16

D_MODEL = 1024
N_HEADS = 8
HEAD_DIM = 64
N_KV = 2
Q_PER_KV = 4
L_CMP = 32
S_CMP = 16
CMP_HIDDEN = 128
L_SEL = 64
N_SEL_TOP = 16
WINDOW = 512
Q_BLOCK = 128
FORCE_BONUS = 1.0e4
CONV_CH = 512
CONV_W = 31
X_HEADS = 4
X_HEAD_DIM = 128
N_GROUPS = 4
EXP_PER_GROUP = 8
N_EXPERTS = 32
TOP_IN_GROUP = 2
D_EXPERT = 512
PAGE_SIZE = 128
EPS = 1e-6

LANES = 128
NEG = -1e30
TINY = float(np.finfo(np.float32).tiny)
VMEM_LIMIT = 56 * 1024 * 1024

C_Q, C_KVC, C_KVS, C_KVW, C_GLU_A, C_GLU_B, C_QX, C_GATE, C_END = 0, 512, 768, 1024, 1280, 1792, 2304, 2816, 2944

TM_PROJ = 256
TM_TAIL = 256
SEL_CHUNK = 512
CMP_PAGES = 32
SAMPLE_PAGES = 16
T_SAMPLE_PAD = 8


def _cparams(*sem):
    return pltpu.CompilerParams(dimension_semantics=sem, vmem_limit_bytes=VMEM_LIMIT)


def _dot(a, b):
    return jnp.dot(a, b, preferred_element_type=F32)


def _dot_nt(a, b):
    return lax.dot_general(a, b, (((1,), (1,)), ((), ())), preferred_element_type=F32)


def _split(a):
    hi = a.astype(BF16)
    lo = (a - hi.astype(F32)).astype(BF16)
    return hi, lo


def _dot_hl(a, s):
    hi, lo = _split(a)
    return _dot(hi, s) + _dot(lo, s)


def _dot3(a, b, nt=False):
    dot = _dot_nt if nt else _dot
    ah, al = _split(a)
    bh, bl = _split(b)
    m = a.shape[0]
    top = dot(jnp.concatenate([ah, al], axis=0), bh)
    return top[:m] + top[m:] + dot(ah, bl)


def _mm(a, w_ref, rows=slice(None)):
    if w_ref.shape[0] == 1:
        return _dot(a.astype(BF16), w_ref[0, rows])
    ah, al = _split(a)
    m = a.shape[0]
    top = _dot(jnp.concatenate([ah, al], axis=0), w_ref[0, rows])
    return top[:m] + top[m:] + _dot(ah, w_ref[1, rows])


def _const_spec(shape):
    return pl.BlockSpec(shape, lambda *_: (0,) * len(shape), pipeline_mode=pl.Buffered(1))


def _rms(x, g):
    return x * lax.rsqrt(jnp.mean(x * x, axis=-1, keepdims=True) + EPS) * g


def _segnorm64(v, seg, segt, gain):
    ss = _dot_hl(v * v, seg)
    inv = lax.rsqrt(ss * (1.0 / HEAD_DIM) + EPS)
    return v * _dot_hl(inv, segt) * gain


def _masked_softmax(s, mask):
    sm = jnp.where(mask, s, NEG)
    m = jnp.max(sm, axis=-1, keepdims=True)
    e = jnp.where(mask, jnp.exp(sm - m), 0.0)
    return e / jnp.maximum(jnp.sum(e, axis=-1, keepdims=True), TINY)


def _iota(shape, dim):
    return lax.broadcasted_iota(jnp.int32, shape, dim)


def _seg_mats(width):
    n = width // HEAD_DIM
    s = np.zeros((width, LANES), np.float32)
    s[np.arange(width), np.arange(width) // HEAD_DIM] = 1.0
    assert n <= LANES
    return jnp.asarray(s, BF16), jnp.asarray(s.T.copy(), BF16)


def _proj_in_kernel(x_ref, nm_ref, w_ref, qg_ref, kgs_ref, kgw_ref, xg_ref, s8_ref, s8t_ref, s2_ref, s2t_ref,
                    q_ref, kvc_ref, kvs_ref, kvsb_ref, kvw_ref, kvwb_ref, gate_ref, u_ref, qx_ref):
    y = _mm(_rms(x_ref[...], nm_ref[...]), w_ref)
    q_ref[...] = _segnorm64(y[:, C_Q:C_KVC], s8_ref[...], s8t_ref[...], qg_ref[...])
    kvc_ref[...] = y[:, C_KVC:C_KVS]
    for c0, gain_ref, o_ref, ob_ref in ((C_KVS, kgs_ref, kvs_ref, kvsb_ref), (C_KVW, kgw_ref, kvw_ref, kvwb_ref)):
        kn = _segnorm64(y[:, c0:c0 + 128], s2_ref[...], s2t_ref[...], gain_ref[...])
        v = y[:, c0 + 128:c0 + 256]
        o_ref[:, 0:128] = kn
        o_ref[:, 128:256] = v
        ob_ref[:, 0:128] = kn.astype(BF16)
        ob_ref[:, 128:256] = v.astype(BF16)
    gate_ref[...] = jax.nn.sigmoid(y[:, C_GATE:C_END])
    u_ref[...] = y[:, C_GLU_A:C_GLU_B] * jax.nn.sigmoid(y[:, C_GLU_B:C_QX])
    xg = xg_ref[...]
    for h in range(X_HEADS):
        seg = y[:, C_QX + 128 * h:C_QX + 128 * (h + 1)]
        qx_ref[:, 128 * h:128 * (h + 1)] = _rms(seg, xg)


def _proj_in(x, nm, w, qg, kgs, kgw, xg):
    M = x.shape[0]
    tm = min(TM_PROJ, M)
    s8, s8t = _seg_mats(512)
    s2, s2t = _seg_mats(128)
    row = lambda i: (i, 0)
    fix = lambda i: (0, 0)
    widths = [(512, F32), (256, F32), (256, F32), (256, BF16), (256, F32), (256, BF16), (128, F32), (512, F32), (512, F32)]
    return pl.pallas_call(
        _proj_in_kernel,
        grid=(M // tm,),
        in_specs=[pl.BlockSpec((tm, D_MODEL), row), pl.BlockSpec((1, D_MODEL), fix),
                  _const_spec(w.shape), pl.BlockSpec((1, 512), fix), pl.BlockSpec((1, 128), fix),
                  pl.BlockSpec((1, 128), fix), pl.BlockSpec((1, 128), fix),
                  pl.BlockSpec((512, LANES), fix), pl.BlockSpec((LANES, 512), fix),
                  pl.BlockSpec((128, LANES), fix), pl.BlockSpec((LANES, 128), fix)],
        out_specs=[pl.BlockSpec((tm, wd), row) for wd, _ in widths],
        out_shape=[jax.ShapeDtypeStruct((M, wd), dt) for wd, dt in widths],
        compiler_params=_cparams("parallel"),
        name="proj_in",
    )(x, nm, w, qg, kgs, kgw, xg, s8, s8t, s2, s2t)


def _cmp_matmul_kernel(pt_ref, *refs):
    del pt_ref
    page_refs, w_ref, o_ref = refs[:-2], refs[-2], refs[-1]
    rows = jnp.concatenate([r[...] for r in page_refs], axis=0).astype(BF16)
    o_ref[...] = _dot(rows, w_ref[...])


def _cmp_matmul(pages, page_ids, w1big):
    n = page_ids.shape[0]
    P = CMP_PAGES
    page_spec = lambda k: pl.BlockSpec((None, 8, 4096), lambda i, pt: (pt[i * P + k], 0, 0))
    return pl.pallas_call(
        _cmp_matmul_kernel,
        grid_spec=pltpu.PrefetchScalarGridSpec(
            num_scalar_prefetch=1,
            grid=(n // P,),
            in_specs=[page_spec(k) for k in range(P)] + [pl.BlockSpec((4096, 1024), lambda i, pt: (0, 0))],
            out_specs=pl.BlockSpec((P * 8, 1024), lambda i, pt: (i, 0)),
        ),
        out_shape=jax.ShapeDtypeStruct((n * 8, 1024), F32),
        compiler_params=_cparams("arbitrary"),
        name="cmp_matmul",
    )(page_ids, *([pages] * P), w1big)


def _cmp_matmul_t_kernel(pt_ref, *refs):
    del pt_ref
    P = CMP_PAGES
    page_refs, w_ref, o_ref, rows_ref = refs[:P], refs[P], refs[P + 1], refs[P + 2]
    for k, r in enumerate(page_refs):
        rows = r[...].T
        for half in range(2):
            rows_ref[half, k * PAGE_SIZE:(k + 1) * PAGE_SIZE, :] = rows[:, half * LANES:(half + 1) * LANES]
    n_chunk = P * PAGE_SIZE // S_CMP
    lhs = jnp.concatenate([rows_ref[half, pl.ds(p, n_chunk, stride=S_CMP), :].astype(BF16)
                           for p in range(S_CMP) for half in range(2)], axis=1)
    o_ref[...] = _dot(lhs, w_ref[...])


def _cmp_matmul_t(pages_t, layer, page_ids, w1big):
    n = page_ids.shape[0]
    P = CMP_PAGES
    page_spec = lambda k: pl.BlockSpec((None, None, 256, PAGE_SIZE), lambda i, pt: (layer, pt[i * P + k], 0, 0))
    return pl.pallas_call(
        _cmp_matmul_t_kernel,
        grid_spec=pltpu.PrefetchScalarGridSpec(
            num_scalar_prefetch=1,
            grid=(n // P,),
            in_specs=[page_spec(k) for k in range(P)] + [pl.BlockSpec((4096, 1024), lambda i, pt: (0, 0))],
            out_specs=pl.BlockSpec((P * 8, 1024), lambda i, pt: (i, 0)),
            scratch_shapes=[pltpu.VMEM((2, P * PAGE_SIZE, LANES), F32)],
        ),
        out_shape=jax.ShapeDtypeStruct((n * 8, 1024), F32),
        compiler_params=_cparams("arbitrary"),
        name="cmp_matmul_t",
    )(page_ids, *([pages_t] * P), w1big)


def _gelu_tanh(x):
    return 0.5 * x * (1.0 + jnp.tanh(np.float32(np.sqrt(2.0 / np.pi)) * (x + 0.044715 * (x * x * x))))


def _cmp_finish_kernel(fs_ref, pos_ref, w1f_ref, w1fl_ref, w2_ref, kg_ref, s2_ref, s2t_ref, kc_ref, vc_ref):
    nc = fs_ref.shape[0]
    ph, plo = _split(pos_ref[...])
    pb = _dot(ph, w1f_ref[...]) + _dot(plo, w1f_ref[...]) + _dot(ph, w1fl_ref[...])
    lane = _iota((1, 512), 1)
    pos_bias = jnp.where(lane < 256, jnp.tile(pb[0:1], (1, 4)), jnp.tile(pb[1:2], (1, 4)))
    first = fs_ref[:, 0:512]
    second = pltpu.roll(fs_ref[:, 512:1024], nc - 1, 0)
    hid = _gelu_tanh(first + second + pos_bias)
    out = _dot(hid.astype(BF16), w2_ref[...])
    kc_ref[...] = _segnorm64(out[:, 0:128], s2_ref[...], s2t_ref[...], kg_ref[...])
    vc_ref[...] = out[:, 128:256]


def _cmp_finish(fs, nb, pos2, w1flat, w1flat_lo, w2big, kg):
    nc = fs.shape[0] // nb
    s2, s2t = _seg_mats(128)
    fix = lambda b: (0, 0)
    return pl.pallas_call(
        _cmp_finish_kernel,
        grid=(nb,),
        in_specs=[pl.BlockSpec((nc, 1024), lambda b: (b, 0)), pl.BlockSpec((8, 4096), fix),
                  pl.BlockSpec((4096, 128), fix), pl.BlockSpec((4096, 128), fix), pl.BlockSpec((512, 256), fix),
                  pl.BlockSpec((1, 128), fix), pl.BlockSpec((128, LANES), fix), pl.BlockSpec((LANES, 128), fix)],
        out_specs=[pl.BlockSpec((None, nc, 128), lambda b: (b, 0, 0))] * 2,
        out_shape=[jax.ShapeDtypeStruct((nb, nc, 128), F32)] * 2,
        compiler_params=_cparams("parallel"),
        name="cmp_finish",
    )(fs, pos2, w1flat, w1flat_lo, w2big, kg, s2, s2t)


def _qpad_group(q, g, dtype=BF16):
    lane = _iota((q.shape[0], LANES), 1)
    keep = (lane < HEAD_DIM) if g == 0 else (lane >= HEAD_DIM)
    parts = []
    for r in range(Q_PER_KV):
        h = Q_PER_KV * g + r
        x = q[:, 128 * (h // 2):128 * (h // 2 + 1)]
        if h % 2 != g:
            x = pltpu.roll(x, HEAD_DIM, 1)
        parts.append(jnp.where(keep, x * (HEAD_DIM ** -0.5), 0.0))
    return jnp.concatenate(parts, axis=0).astype(dtype)


def _slopes(rows_per_head, g):
    r = _iota((Q_PER_KV * rows_per_head, 1), 0) >> (rows_per_head.bit_length() - 1)
    out = jnp.zeros(r.shape, F32)
    for k in range(Q_PER_KV):
        out = jnp.where(r == k, np.float32(2.0 ** (-(Q_PER_KV * g + k + 1))), out)
    return out


def _tile4(x):
    return jnp.concatenate([x] * Q_PER_KV, axis=0)


def _top_select(score, n_top):
    lane = _iota(score.shape, 1).astype(F32)
    big = np.float32(score.shape[1])
    sel = jnp.zeros(score.shape, F32)
    sc = score
    for _ in range(n_top):
        m = jnp.max(sc, axis=-1, keepdims=True)
        idx = jnp.min(jnp.where(sc == m, lane, big), axis=-1, keepdims=True)
        pick = lane == idx
        sel = jnp.where(pick, 1.0, sel)
        sc = jnp.where(pick, -3.0e38, sc)
    return sel


def _select_blocks(p, cover, qpos, n_blk_real):
    R = qpos.shape[0]
    psum = p[0:R] + p[R:2 * R] + p[2 * R:3 * R] + p[3 * R:4 * R]
    imp = _dot_hl(psum, cover)
    blk = _iota(imp.shape, 1)
    cur = qpos >> 6
    valid = (blk * L_SEL <= qpos) & (blk < n_blk_real)
    forced = (blk == 0) | (blk == cur) | (blk == cur - 1)
    score = jnp.where(valid, imp + jnp.where(forced, FORCE_BONUS, 0.0), -FORCE_BONUS)
    score = jnp.where(blk < n_blk_real, score, -2.0e38)
    sel = _top_select(score, N_SEL_TOP)
    return jnp.where(valid, sel, 0.0)


def _merge_heads(gates, oc, os_, ow, R):
    lane = _iota((R, LANES), 1)
    tiles = []
    for t in range(N_HEADS // 2):
        g = t // 2
        halves = []
        for h in (2 * t, 2 * t + 1):
            r = h % Q_PER_KV
            sl = slice(r * R, (r + 1) * R)
            o = (gates[:, 3 * h:3 * h + 1] * oc[g][sl] + gates[:, 3 * h + 1:3 * h + 2] * os_[g][sl]
                 + gates[:, 3 * h + 2:3 * h + 3] * ow[g][sl])
            halves.append(o)
        lo, hi = halves
        if g == 0:
            hi = pltpu.roll(hi, HEAD_DIM, 1)
        else:
            lo = pltpu.roll(lo, HEAD_DIM, 1)
        tiles.append(jnp.where(lane < HEAD_DIM, lo, hi))
    return jnp.concatenate(tiles, axis=1)


MASK_BIAS = -1e30
X_HI, X_LO, X_CHUNK, X_ROW = 8, 9, 10, 11
X_ONE = X_CHUNK


def _ext_cols(lane4, cols):
    out = jnp.zeros(lane4.shape, F32)
    for k, v in cols:
        out = jnp.where(lane4 == k, v, out)
    return out


def _top_select_t(score, n_top):
    row = _iota(score.shape, 0).astype(F32)
    big = np.float32(score.shape[0])
    sel = jnp.zeros(score.shape, F32)
    sc = score
    for _ in range(n_top):
        m = jnp.max(sc, axis=0, keepdims=True)
        idx = jnp.min(jnp.where(sc == m, row, big), axis=0, keepdims=True)
        pick = row == idx
        sel = jnp.where(pick, 1.0, sel)
        sc = jnp.where(pick, -3.0e38, sc)
    return sel


def _nsa_prompt_kernel(q_ref, g_ref, kc_ref, vc_ref, kvs_ref, kvw_ref, covt_ref, xc_ref, xs_ref, xw_ref, o_ref):
    R = Q_BLOCK
    i = pl.program_id(1)
    start = i * R
    startf = start.astype(F32)
    q = q_ref[...]
    qloc = _iota((R, 1), 0)
    n_cmp = kc_ref.shape[0]
    nb = covt_ref.shape[0]
    last = (start + R - 1) // SEL_CHUNK
    k_last = pl.multiple_of(last * SEL_CHUNK, SEL_CHUNK)
    ws = pl.multiple_of(jnp.maximum(start - WINDOW, 0), R)
    woff = start - ws
    bias_c = jnp.where(start + qloc - (_iota((1, n_cmp), 1) * S_CMP + (L_CMP - 1)) >= 0, 0.0, MASK_BIAS)
    dw = woff + qloc - _iota((1, WINDOW + R), 1)
    bias_w = jnp.where((dw >= 0) & (dw < WINDOW), 0.0, MASK_BIAS)
    bias_d = jnp.where(start + qloc - (k_last + _iota((1, SEL_CHUNK), 1)) >= 0, 0.0, MASK_BIAS)
    has_cmp = jnp.where(start + qloc >= L_CMP - 1, 1.0, 0.0)
    qloc4 = _tile4(qloc.astype(F32))
    lane4 = _iota((Q_PER_KV * R, LANES), 1)
    blk = _iota((nb, 1), 0)
    qpos_row = start + _iota((1, R), 1)
    cur = qpos_row >> 6
    valid_t = blk * L_SEL <= qpos_row
    forced_t = (blk == 0) | (blk == cur) | (blk == cur - 1)
    HR = Q_PER_KV * R
    qg = jnp.concatenate([_qpad_group(q, 0), _qpad_group(q, 1)], axis=0)
    slope = jnp.concatenate([_slopes(R, 0), _slopes(R, 1)], axis=0)
    qloc8 = jnp.concatenate([qloc4, qloc4], axis=0)
    lane8 = jnp.concatenate([lane4, lane4], axis=0)

    ext = _ext_cols(lane8, [(0, slope * 1024.0), (1, slope * 16.0), (2, -slope * startf),
                            (3, -slope * (qloc8 - (L_CMP - 1.0)))])
    lhs = jnp.concatenate([qg, ext.astype(BF16)], axis=1)
    s = _dot_nt(lhs, jnp.concatenate([kc_ref[...].astype(BF16), xc_ref[...]], axis=1))
    ps, psums = [], []
    for h in range(N_HEADS):
        sr = s[h * R:(h + 1) * R] + bias_c
        e = jnp.exp(sr - jnp.max(sr, axis=-1, keepdims=True))
        p = e * (has_cmp / jnp.sum(e, axis=-1, keepdims=True))
        ps.append(p.astype(BF16))
        if h % Q_PER_KV == 0:
            psums.append(p)
        else:
            psums[-1] = psums[-1] + p
    oc = _dot(jnp.concatenate(ps, axis=0), vc_ref[...].astype(BF16))
    selb = []
    for g in range(N_KV):
        ph, plo = _split(psums[g])
        imp_t = _dot_nt(covt_ref[...], ph) + _dot_nt(covt_ref[...], plo)
        score = jnp.where(valid_t, imp_t + jnp.where(forced_t, FORCE_BONUS, 0.0), -FORCE_BONUS)
        sel_t = _top_select_t(score, N_SEL_TOP)
        selb.append(jnp.where(valid_t & (sel_t > 0.5), 0.0, MASK_BIAS).T)

    kvw = kvw_ref[pl.ds(ws, WINDOW + R), :]
    ext = _ext_cols(lane8, [(X_HI, slope * 256.0), (X_LO, slope), (X_CHUNK, -slope * woff.astype(F32)),
                            (X_ROW, -slope * qloc8)])
    lhs = jnp.concatenate([qg, ext.astype(BF16)], axis=1)
    s = _dot_nt(lhs, jnp.concatenate([kvw[:, 0:128], xw_ref[...]], axis=1))
    es = []
    for h in range(N_HEADS):
        sr = s[h * R:(h + 1) * R] + bias_w
        es.append(jnp.exp(sr - jnp.max(sr, axis=-1, keepdims=True)).astype(BF16))
    acc = _dot(jnp.concatenate(es, axis=0), jnp.concatenate([kvw[:, 128:256], xw_ref[...]], axis=1))
    ow = acc[:, 0:LANES] / acc[:, LANES + X_ONE:LANES + X_ONE + 1]

    ext_base = _ext_cols(lane8, [(X_HI, slope * 256.0), (X_LO, slope), (X_ROW, -slope * qloc8)])

    def scores(c):
        k0 = pl.multiple_of(c * SEL_CHUNK, SEL_CHUNK)
        shift = lax.rem(nb - 8 * c, nb)
        sel8 = jnp.concatenate([_tile4(pltpu.roll(selb[0], shift, 1)), _tile4(pltpu.roll(selb[1], shift, 1))], axis=0)
        ext_l = jnp.where(lane8 < 8, sel8,
                          jnp.where(lane8 == X_CHUNK, slope * (k0 - start).astype(F32), ext_base))
        lhs = jnp.concatenate([qg, ext_l.astype(BF16)], axis=1)
        return _dot_nt(lhs, jnp.concatenate([kvs_ref[pl.ds(k0, SEL_CHUNK), 0:128], xs_ref[...]], axis=1))

    def absorb(c, sc, m, acc, diag):
        k0 = pl.multiple_of(c * SEL_CHUNK, SEL_CHUNK)
        es, ms = [], []
        for h in range(N_HEADS):
            rows = slice(h * R, (h + 1) * R)
            sr = sc[rows]
            if diag:
                sr = sr + bias_d
            m_new = jnp.maximum(m[rows], jnp.max(sr, axis=-1, keepdims=True))
            es.append(jnp.exp(sr - m_new).astype(BF16))
            ms.append(m_new)
        m_new = jnp.concatenate(ms, axis=0)
        v_aug = jnp.concatenate([kvs_ref[pl.ds(k0, SEL_CHUNK), 128:256], xs_ref[...]], axis=1)
        return m_new, jnp.exp(m - m_new) * acc + _dot(jnp.concatenate(es, axis=0), v_aug)

    def body(c, carry):
        sc, m, acc = carry
        sc_next = scores(c + 1)
        m, acc = absorb(c, sc, m, acc, False)
        return sc_next, m, acc

    init = (scores(0), jnp.full((2 * HR, 1), MASK_BIAS, F32), jnp.zeros((2 * HR, 2 * LANES), F32))
    sc, m, acc = lax.fori_loop(0, last, body, init)
    _, acc = absorb(last, sc, m, acc, True)
    os_ = acc[:, 0:LANES] / acc[:, LANES + X_ONE:LANES + X_ONE + 1]
    split = lambda a: [a[0:HR], a[HR:2 * HR]]
    o_ref[...] = _merge_heads(g_ref[...], split(oc), split(os_), split(ow), R)


def _pos_ext(n, sel_shift=None):
    j = np.arange(n)
    x = np.zeros((n, LANES), np.float32)
    if sel_shift is not None:
        x[j, (j >> sel_shift) & 7] = 1.0
    x[:, X_HI] = j >> 8
    x[:, X_LO] = j & 255
    x[:, X_CHUNK] = 1.0
    x[:, X_ROW] = 1.0
    return jnp.asarray(x, BF16)


def _nsa_prompt(q, gates, kc, vc, kvs_b, kvw_b):
    B, T, _ = q.shape
    n_cmp = kc.shape[1]
    n_sel = T // L_SEL
    assert n_sel <= LANES and T % SEL_CHUNK == 0
    start = np.arange(n_cmp)[None, :] * S_CMP
    j = np.arange(LANES)[:, None]
    cover_t = ((start < (j + 1) * L_SEL) & (start + L_CMP > j * L_SEL) & (np.arange(n_cmp)[None, :] < T // S_CMP - 1)
               & (j < n_sel))
    n = np.arange(n_cmp)
    xc = np.zeros((n_cmp, LANES), np.float32)
    xc[:, 0], xc[:, 1], xc[:, 2], xc[:, 3] = n >> 6, n & 63, 1.0, 1.0
    blk = lambda b, i: (b, i, 0)
    per_b = lambda b, i: (b, 0, 0)
    fix = lambda b, i: (0, 0)
    return pl.pallas_call(
        _nsa_prompt_kernel,
        grid=(B, T // Q_BLOCK),
        in_specs=[pl.BlockSpec((None, Q_BLOCK, 512), blk), pl.BlockSpec((None, Q_BLOCK, 128), blk),
                  pl.BlockSpec((None, n_cmp, 128), per_b), pl.BlockSpec((None, n_cmp, 128), per_b),
                  pl.BlockSpec((None, T, 256), per_b), pl.BlockSpec((None, T, 256), per_b),
                  pl.BlockSpec((LANES, n_cmp), fix), pl.BlockSpec((n_cmp, LANES), fix),
                  pl.BlockSpec((SEL_CHUNK, LANES), fix), pl.BlockSpec((WINDOW + Q_BLOCK, LANES), fix)],
        out_specs=pl.BlockSpec((None, Q_BLOCK, 512), blk),
        out_shape=jax.ShapeDtypeStruct((B, T, 512), F32),
        compiler_params=_cparams("parallel", "arbitrary"),
        name="nsa_prompt",
    )(q, gates, kc, vc, kvs_b, kvw_b, jnp.asarray(cover_t, BF16), jnp.asarray(xc, BF16),
      _pos_ext(SEL_CHUNK, 6), _pos_ext(WINDOW + Q_BLOCK))


def _nsa_sample_kernel(pt_ref, q_ref, g_ref, kc_ref, vc_ref, cov_ref, kvn_ref, win_ref, kwn_ref, *rest,
                       past, n_blk_real, precise):
    del pt_ref
    cast = (lambda a: a) if precise else (lambda a: a.astype(BF16))
    mm = _dot3 if precise else _dot
    mm_nt = functools.partial(_dot3, nt=True) if precise else _dot_nt
    P = SAMPLE_PAGES
    page_refs = rest[:P]
    o_ref = rest[P]
    qg_s, selm_s, m_s, l_s, acc_s, oc_s, ow_s = rest[P + 1:]
    R = T_SAMPLE_PAD
    HR = Q_PER_KV * R
    j = pl.program_id(1)
    nj = pl.num_programs(1)
    qpos = past + _iota((R, 1), 0)
    qpos8 = jnp.concatenate([_tile4(qpos)] * N_KV, axis=0)
    slope = jnp.concatenate([_slopes(R, g) for g in range(N_KV)], axis=0)
    n_cmp = kc_ref.shape[0]
    nb = cov_ref.shape[1]
    chunk = P * PAGE_SIZE
    zpad = jnp.zeros((LANES - R, 256), F32)

    @pl.when(j == 0)
    def _():
        q = q_ref[...]
        qg = jnp.concatenate([_qpad_group(q, g, qg_s.dtype) for g in range(N_KV)], axis=0)
        qg_s[...] = qg
        ncol = _iota((1, n_cmp), 1)
        d = qpos8 - (ncol * S_CMP + (L_CMP - 1))
        s = mm_nt(qg, cast(kc_ref[...])) - slope * d.astype(F32)
        p = _masked_softmax(s, (d >= 0) & (ncol < n_cmp - 1))
        oc_s[...] = mm(cast(p), cast(vc_ref[...]))
        for g in range(N_KV):
            selm_s[g] = _select_blocks(p[g * HR:(g + 1) * HR], cov_ref[...], qpos, n_blk_real)
        win = cast(win_ref[...])
        col = _iota((1, WINDOW), 1)
        d_old = qpos8 - (past - WINDOW + col)
        s_old = jnp.where(d_old < WINDOW, mm(qg, win[0:128]) - slope * d_old.astype(F32), NEG)
        kv_new = cast(jnp.concatenate([kwn_ref[...], zpad], axis=0))
        coln = _iota((1, LANES), 1)
        d_new = qpos8 - (past + coln)
        s_new = jnp.where((d_new >= 0) & (coln < 4), mm_nt(qg, kv_new[:, 0:128]) - slope * d_new.astype(F32), NEG)
        m = jnp.maximum(jnp.max(s_old, axis=-1, keepdims=True), jnp.max(s_new, axis=-1, keepdims=True))
        e_old = jnp.where(d_old < WINDOW, jnp.exp(s_old - m), 0.0)
        e_new = jnp.where((d_new >= 0) & (coln < 4), jnp.exp(s_new - m), 0.0)
        den = jnp.sum(e_old, axis=-1, keepdims=True) + jnp.sum(e_new, axis=-1, keepdims=True)
        num = mm_nt(cast(e_old), win[128:256]) + mm(cast(e_new), kv_new[:, 128:256])
        ow_s[...] = num / jnp.maximum(den, TINY)
        m_s[...] = jnp.full(m_s.shape, NEG, F32)
        l_s[...] = jnp.zeros(l_s.shape, F32)
        acc_s[...] = jnp.zeros(acc_s.shape, F32)

    def update(scores, values, kpos, extra_mask):
        nk = kpos.shape[1]
        expand = jnp.where(_iota((nb, nk), 0) == (kpos >> 6), 1.0, 0.0).astype(BF16)
        picked = jnp.concatenate([_tile4(_dot(selm_s[g].astype(BF16), expand)) for g in range(N_KV)], axis=0)
        dd = qpos8 - kpos
        mask = (picked > 0.5) & (dd >= 0)
        if extra_mask is not None:
            mask = mask & extra_mask
        sc = jnp.where(mask, scores(qg_s[...]) - slope * dd.astype(F32), NEG)
        m = m_s[...]
        m_new = jnp.maximum(m, jnp.max(sc, axis=-1, keepdims=True))
        a = jnp.exp(m - m_new)
        e = jnp.where(mask, jnp.exp(sc - m_new), 0.0)
        m_s[...] = m_new
        l_s[...] = a * l_s[...] + jnp.sum(e, axis=-1, keepdims=True)
        acc_s[...] = a * acc_s[...] + values(cast(e))

    k_t = jnp.concatenate([cast(r[0:128, :]) for r in page_refs], axis=1)
    v_t = jnp.concatenate([cast(r[128:256, :]) for r in page_refs], axis=1)
    update(lambda qg: mm(qg, k_t), lambda e: mm_nt(e, v_t), j * chunk + _iota((1, chunk), 1), None)

    @pl.when(j == nj - 1)
    def _():
        kv_new = cast(jnp.concatenate([kvn_ref[...], zpad], axis=0))
        col = _iota((1, LANES), 1)
        update(lambda qg: mm_nt(qg, kv_new[:, 0:128]), lambda e: mm(e, kv_new[:, 128:256]), past + col, col < 4)
        os_ = acc_s[...] / jnp.maximum(l_s[...], TINY)
        split = lambda a: [a[0:HR], a[HR:2 * HR]]
        o_ref[...] = _merge_heads(g_ref[...], split(oc_s[...]), split(os_), split(ow_s[...]), R)


def _nsa_sample(q, gates, kc, vc, kvs_new, win_t, kvw_new, sel_pages_t, layer, page_table, precise=False):
    DB = q.shape[0]
    n_pages = page_table.shape[1]
    past = n_pages * PAGE_SIZE
    P = SAMPLE_PAGES
    R = T_SAMPLE_PAD
    n_cmp = kc.shape[1]
    n_blk_real = -(-(past + 4) // L_SEL)
    nb = -(-n_blk_real // LANES) * LANES
    start = np.arange(n_cmp)[:, None] * S_CMP
    jj = np.arange(nb)[None, :]
    cover = ((start < (jj + 1) * L_SEL) & (start + L_CMP > jj * L_SEL) & (jj < n_blk_real)
             & (np.arange(n_cmp)[:, None] < n_cmp - 1))
    per_b = lambda b, j, pt: (b, 0, 0)
    page_spec = lambda k: pl.BlockSpec((None, None, 256, PAGE_SIZE),
                                       lambda b, j, pt: (layer, pt[b * n_pages + j * P + k], 0, 0))
    kern = functools.partial(_nsa_sample_kernel, past=past, n_blk_real=n_blk_real, precise=precise)
    rows = N_KV * Q_PER_KV * R
    return pl.pallas_call(
        kern,
        grid_spec=pltpu.PrefetchScalarGridSpec(
            num_scalar_prefetch=1,
            grid=(DB, n_pages // P),
            in_specs=[pl.BlockSpec((None, R, 512), per_b), pl.BlockSpec((None, R, 128), per_b),
                      pl.BlockSpec((None, n_cmp, 128), per_b), pl.BlockSpec((None, n_cmp, 128), per_b),
                      pl.BlockSpec((n_cmp, nb), lambda b, j, pt: (0, 0)),
                      pl.BlockSpec((None, R, 256), per_b),
                      pl.BlockSpec((None, None, 256, WINDOW), lambda b, j, pt: (layer, b, 0, 0)),
                      pl.BlockSpec((None, R, 256), per_b)] + [page_spec(k) for k in range(P)],
            out_specs=pl.BlockSpec((None, R, 512), per_b),
            scratch_shapes=[pltpu.VMEM((rows, LANES), F32 if precise else BF16), pltpu.VMEM((N_KV, R, nb), F32),
                            pltpu.VMEM((rows, 1), F32), pltpu.VMEM((rows, 1), F32),
                            pltpu.VMEM((rows, LANES), F32), pltpu.VMEM((rows, LANES), F32),
                            pltpu.VMEM((rows, LANES), F32)],
        ),
        out_shape=jax.ShapeDtypeStruct((DB, R, 512), F32),
        compiler_params=_cparams("parallel", "arbitrary"),
        name="nsa_sample",
    )(page_table.reshape(-1), q, gates, kc, vc, jnp.asarray(cover, BF16), kvs_new, win_t, kvw_new,
      *([sel_pages_t] * P))


def _ln_silu(c, g, b):
    mu = jnp.mean(c, axis=-1, keepdims=True)
    xc = c - mu
    y = xc * lax.rsqrt(jnp.mean(xc * xc, axis=-1, keepdims=True) + EPS) * g + b
    return y * jax.nn.sigmoid(y)


def _conv_prompt_kernel(um_ref, up_ref, w_ref, b_ref, g_ref, bb_ref, o_ref, ext_ref):
    tt = um_ref.shape[0]
    i = pl.program_id(1)
    ext_ref[0:32, :] = jnp.where(i > 0, up_ref[...], 0.0)
    ext_ref[32:32 + tt, :] = um_ref[...]
    acc = jnp.zeros((tt, CONV_CH), F32)
    for j in range(CONV_W):
        acc = acc + ext_ref[pl.ds(j + 2, tt), :] * w_ref[j:j + 1, :]
    o_ref[...] = _ln_silu(acc + b_ref[...], g_ref[...], bb_ref[...])


def _conv_prompt(u, w, b, g, bb):
    B, T, C = u.shape
    tt = 512
    fix = lambda bi, i: (0, 0)
    return pl.pallas_call(
        _conv_prompt_kernel,
        grid=(B, T // tt),
        in_specs=[pl.BlockSpec((None, tt, C), lambda bi, i: (bi, i, 0)),
                  pl.BlockSpec((None, 32, C), lambda bi, i: (bi, jnp.maximum(i * (tt // 32) - 1, 0), 0)),
                  pl.BlockSpec((32, C), fix), pl.BlockSpec((1, C), fix), pl.BlockSpec((1, C), fix),
                  pl.BlockSpec((1, C), fix)],
        out_specs=pl.BlockSpec((None, tt, C), lambda bi, i: (bi, i, 0)),
        out_shape=jax.ShapeDtypeStruct((B, T, C), F32),
        scratch_shapes=[pltpu.VMEM((tt + 32, C), F32)],
        compiler_params=_cparams("parallel", "arbitrary"),
        name="conv_prompt",
    )(u, u, w, b, g, bb)


def _conv_sample_kernel(ext_ref, w_ref, b_ref, g_ref, bb_ref, o_ref):
    n_out = o_ref.shape[0]
    for t in range(n_out):
        acc = jnp.zeros(o_ref.shape[1:], F32)
        for j in range(CONV_W):
            acc = acc + ext_ref[t + j] * w_ref[j:j + 1, :]
        o_ref[t] = _ln_silu(acc + b_ref[...], g_ref[...], bb_ref[...])


def _conv_sample(ext_t, w, b, g, bb, n_out):
    _, DB, C = ext_t.shape
    return pl.pallas_call(
        _conv_sample_kernel,
        out_shape=jax.ShapeDtypeStruct((n_out, DB, C), F32),
        name="conv_sample",
    )(ext_t, w, b, g, bb)


def _mem_kv_kernel(m_ref, g_ref, w_ref, kg_ref, o_ref):
    hb = _rms(m_ref[...], g_ref[...]).astype(BF16)
    kv = _dot(hb, w_ref[...])
    kg = kg_ref[...]
    for h in range(X_HEADS):
        o_ref[:, 128 * h:128 * (h + 1)] = _rms(kv[:, 128 * h:128 * (h + 1)], kg)
    o_ref[:, 512:1024] = kv[:, 512:1024]


def _mem_kv(mem, g, w, kg):
    B, M, D = mem.shape
    fix = lambda b: (0, 0)
    return pl.pallas_call(
        _mem_kv_kernel,
        grid=(B,),
        in_specs=[pl.BlockSpec((None, M, D), lambda b: (b, 0, 0)), pl.BlockSpec((1, D), fix),
                  pl.BlockSpec((D, 1024), fix), pl.BlockSpec((1, 128), fix)],
        out_specs=pl.BlockSpec((None, M, 1024), lambda b: (b, 0, 0)),
        out_shape=jax.ShapeDtypeStruct((B, M, 1024), F32),
        compiler_params=_cparams("parallel"),
        name="mem_kv",
    )(mem, g, w, kg)


def _mem_attend_kernel(q_ref, kv_ref, o_ref, *, precise):
    for h in range(X_HEADS):
        q = q_ref[:, 128 * h:128 * (h + 1)]
        k = kv_ref[:, 128 * h:128 * (h + 1)]
        v = kv_ref[:, 512 + 128 * h:512 + 128 * (h + 1)]
        s = (_dot3(q, k, nt=True) if precise else _dot_nt(q.astype(BF16), k.astype(BF16))) * (X_HEAD_DIM ** -0.5)
        e = jnp.exp(s - jnp.max(s, axis=-1, keepdims=True))
        p = e / jnp.sum(e, axis=-1, keepdims=True)
        o_ref[:, 128 * h:128 * (h + 1)] = _dot3(p, v) if precise else _dot(p.astype(BF16), v.astype(BF16))


def _mem_attend(qx, mkv, tq, precise=False):
    B, T, _ = qx.shape
    M = mkv.shape[1]
    return pl.pallas_call(
        functools.partial(_mem_attend_kernel, precise=precise),
        grid=(B, T // tq),
        in_specs=[pl.BlockSpec((None, tq, 512), lambda b, i: (b, i, 0)),
                  pl.BlockSpec((None, M, 1024), lambda b, i: (b, 0, 0))],
        out_specs=pl.BlockSpec((None, tq, 512), lambda b, i: (b, i, 0)),
        out_shape=jax.ShapeDtypeStruct((B, T, 512), F32),
        compiler_params=_cparams("parallel", "arbitrary"),
        name="mem_attend",
    )(qx, mkv)


def _tail_kernel(x_ref, on_ref, ocv_ref, om_ref, nm_ref, wb_ref, wg_ref, bg_ref, wo_ref, nf_ref,
                 wrh_ref, wrl_ref, br_ref, x1_ref, xn_ref, route_ref):
    x = x_ref[...]
    gate = jax.nn.sigmoid(_mm(_rms(x, nm_ref[...]), wg_ref) + bg_ref[...])
    y_nsa = _mm(on_ref[...], wb_ref, slice(0, 512))
    y_conv = _mm(ocv_ref[...], wb_ref, slice(512, 1024))
    y_mem = _mm(om_ref[...], wb_ref, slice(1024, 1536))
    merged = gate[:, 0:1024] * y_nsa + gate[:, 1024:2048] * y_conv + gate[:, 2048:3072] * y_mem
    x1 = x + _mm(merged, wo_ref)
    x1_ref[...] = x1
    xn = _rms(x1, nf_ref[...])
    xn_ref[...] = xn
    xh, xl = _split(xn)
    logits = _dot(xh, wrh_ref[...]) + _dot(xl, wrh_ref[...]) + _dot(xh, wrl_ref[...]) + br_ref[...]
    lane = _iota(logits.shape, 1)
    lanef = lane.astype(F32)
    is_g = lane < N_GROUPS
    gl = jnp.where(is_g, logits, NEG)
    gm = jnp.max(gl, axis=-1, keepdims=True)
    p_g = 1.0 / jnp.sum(jnp.where(is_g, jnp.exp(gl - gm), 0.0), axis=-1, keepdims=True)
    g_sel = jnp.min(jnp.where(gl == gm, lanef, 128.0), axis=-1, keepdims=True).astype(jnp.int32)
    lo = N_GROUPS + EXP_PER_GROUP * g_sel
    el = jnp.where((lane >= lo) & (lane < lo + EXP_PER_GROUP), logits, NEG)
    m0 = jnp.max(el, axis=-1, keepdims=True)
    i0 = jnp.min(jnp.where(el == m0, lanef, 128.0), axis=-1, keepdims=True)
    el = jnp.where(lanef == i0, 2.0 * NEG, el)
    m1 = jnp.max(el, axis=-1, keepdims=True)
    i1 = jnp.min(jnp.where(el == m1, lanef, 128.0), axis=-1, keepdims=True)
    t = jnp.exp(m1 - m0)
    w0 = p_g / (1.0 + t)
    w1 = p_g * t / (1.0 + t)
    route_ref[...] = jnp.where(lane == 0, i0 - N_GROUPS, jnp.where(lane == 1, i1 - N_GROUPS,
                               jnp.where(lane == 2, w0, jnp.where(lane == 3, w1, 0.0))))


def _tail(x, o_nsa, o_conv, o_mem, nm, wb, wg, bg, wo, nf, wrh, wrl, br):
    M = x.shape[0]
    tm = min(TM_TAIL, M)
    row = lambda i: (i, 0)
    fix = lambda i: (0, 0)
    return pl.pallas_call(
        _tail_kernel,
        grid=(M // tm,),
        in_specs=[pl.BlockSpec((tm, D_MODEL), row), pl.BlockSpec((tm, 512), row), pl.BlockSpec((tm, 512), row),
                  pl.BlockSpec((tm, 512), row), pl.BlockSpec((1, D_MODEL), fix), _const_spec(wb.shape),
                  _const_spec(wg.shape), pl.BlockSpec((1, 3072), fix), _const_spec(wo.shape),
                  pl.BlockSpec((1, D_MODEL), fix), pl.BlockSpec((D_MODEL, LANES), fix), pl.BlockSpec((D_MODEL, LANES), fix),
                  pl.BlockSpec((1, LANES), fix)],
        out_specs=[pl.BlockSpec((tm, D_MODEL), row), pl.BlockSpec((tm, D_MODEL), row), pl.BlockSpec((tm, LANES), row)],
        out_shape=[jax.ShapeDtypeStruct((M, D_MODEL), F32), jax.ShapeDtypeStruct((M, D_MODEL), F32),
                   jax.ShapeDtypeStruct((M, LANES), F32)],
        compiler_params=_cparams("parallel"),
        name="tail",
    )(x, o_nsa, o_conv, o_mem, nm, wb, wg, bg, wo, nf, wrh, wrl, br)


def _stack(w, precise):
    hi = w.astype(BF16)
    if not precise:
        return hi[None]
    return jnp.stack([hi, (w - hi.astype(F32)).astype(BF16)])


def _precise_weights(l, p, w):
    out = dict(w)
    out.update(w_in=_stack(w['w_in_f32'], True), wb=_stack(p['w_branch'][l], True), wg=_stack(p['w_gate'][l], True),
               wo=_stack(p['w_out'][l], True), w_up=_stack(p['w_up'][l], True), w_down=_stack(p['w_down'][l], True))
    return out


def _layer_weights(l, p):
    w = p['w_in'][l]
    n_gate = 3 * N_HEADS
    w_re = jnp.concatenate([w[:, :1280], w[:, 1280 + n_gate:], w[:, 1280:1280 + n_gate],
                            jnp.zeros((D_MODEL, C_END - C_GATE - n_gate), F32)], axis=1)
    w1 = p['w_cmp1'][l]
    eye = jnp.eye(2, dtype=F32)
    halves = []
    for half in range(2):
        wh = w1[:, S_CMP * half:S_CMP * (half + 1)]
        big = jnp.einsum('kpdf,kK,gG->pkgdKGf', wh, eye, eye)
        halves.append(big.reshape(S_CMP * 256, 4 * CMP_HIDDEN))
    w1big = jnp.concatenate(halves, axis=1).astype(BF16)
    w1flat = jnp.concatenate([w1[0].reshape(L_CMP * HEAD_DIM, CMP_HIDDEN), w1[1].reshape(L_CMP * HEAD_DIM, CMP_HIDDEN)], axis=0)
    pos = p['pos_cmp'][l].reshape(2, L_CMP * HEAD_DIM)
    pos2 = jnp.zeros((8, 4096), F32).at[0, :2048].set(pos[0]).at[1, 2048:].set(pos[1])
    w1f_hi = w1flat.astype(BF16)
    w1f_lo = (w1flat - w1f_hi.astype(F32)).astype(BF16)
    w2 = p['w_cmp2'][l]
    w2big = jnp.einsum('kfd,kK,gG->kgfKGd', w2, eye, eye).reshape(4 * CMP_HIDDEN, 256).astype(BF16)
    wr = jnp.concatenate([p['w_grp'][l], p['w_route'][l], jnp.zeros((D_MODEL, LANES - N_GROUPS - N_EXPERTS), F32)], axis=1)
    wr_hi = wr.astype(BF16)
    wr_lo = (wr - wr_hi.astype(F32)).astype(BF16)
    br = jnp.concatenate([p['b_grp'][l], p['b_route'][l], jnp.zeros((LANES - N_GROUPS - N_EXPERTS,), F32)]).reshape(1, LANES)
    cw = jnp.concatenate([p['conv_w'][l], jnp.zeros((1, CONV_CH), F32)], axis=0)
    r1 = lambda a: a.reshape(1, -1)
    return dict(
        nm=r1(p['norm_mix'][l]), w_in=_stack(w_re, False), w_in_f32=w_re, qg=r1(jnp.tile(p['q_gain'][l], N_HEADS)),
        kg_cmp=r1(jnp.tile(p['k_gain'][l, 0], 2)), kg_sel=r1(jnp.tile(p['k_gain'][l, 1], 2)),
        kg_win=r1(jnp.tile(p['k_gain'][l, 2], 2)), xq_gain=r1(p['mem_qk_gain'][l, 0]), xk_gain=r1(p['mem_qk_gain'][l, 1]),
        w1big=w1big, pos2=pos2, w1f_hi=w1f_hi, w1f_lo=w1f_lo, w2big=w2big,
        conv_w=cw, conv_b=r1(p['conv_b'][l]), ln_g=r1(p['conv_ln_g'][l]), ln_b=r1(p['conv_ln_b'][l]),
        mem_norm=r1(p['mem_norm'][l]), w_mem=p['w_mem_kv'][l].astype(BF16),
        wb=_stack(p['w_branch'][l], False), wg=_stack(p['w_gate'][l], False), bg=r1(p['b_gate'][l]),
        wo=_stack(p['w_out'][l], False), nf=r1(p['norm_ffn'][l]), wr_hi=wr_hi, wr_lo=wr_lo, br=br,
        w_up=_stack(p['w_up'][l], False), w_down=_stack(p['w_down'][l], False),
    )


FFN_BLOCK = 256
ROW_TILE = 256
DMA_UNROLL = 8


def _rank_kernel(route_ref, tri_ref, rank_ref, cnt_ref, base_ref):
    @pl.when(pl.program_id(0) == 0)
    def _():
        base_ref[...] = jnp.zeros(base_ref.shape, F32)

    route = route_ref[...]
    lane = _iota(route.shape, 1)
    lanef = lane.astype(F32)
    o0 = jnp.where(lanef == route[:, 0:1], 1.0, 0.0)
    o1 = jnp.where(lanef == route[:, 1:2], 1.0, 0.0)
    both = o0 + o1
    before = _dot(tri_ref[...], both.astype(BF16)) + base_ref[0:1, :]
    r0 = jnp.sum(o0 * before, axis=-1, keepdims=True)
    r1 = jnp.sum(o1 * (before + o0), axis=-1, keepdims=True)
    rank_ref[...] = jnp.where(lane == 0, r0, jnp.where(lane == 1, r1, 0.0))
    total = base_ref[0:1, :] + jnp.sum(both, axis=0, keepdims=True)
    base_ref[...] = jnp.broadcast_to(total, base_ref.shape)
    cnt_ref[...] = jnp.broadcast_to(total, cnt_ref.shape)


def _rank(route):
    N = route.shape[0]
    tr = min(512, N)
    tri = jnp.asarray(np.tril(np.ones((tr, tr), np.float32), -1), BF16)
    return pl.pallas_call(
        _rank_kernel,
        grid=(N // tr,),
        in_specs=[pl.BlockSpec((tr, LANES), lambda i: (i, 0)), pl.BlockSpec((tr, tr), lambda i: (0, 0))],
        out_specs=[pl.BlockSpec((tr, LANES), lambda i: (i, 0)), pl.BlockSpec((8, LANES), lambda i: (0, 0))],
        out_shape=[jax.ShapeDtypeStruct((N, LANES), F32), jax.ShapeDtypeStruct((8, LANES), F32)],
        scratch_shapes=[pltpu.VMEM((8, LANES), F32)],
        compiler_params=_cparams("arbitrary"),
        name="moe_rank",
    )(route, tri)


def _row_dma_loop(n, make_copy):
    def start(j, c):
        make_copy(j).start()
        return c

    def wait(j, c):
        make_copy(j).wait()
        return c

    lax.fori_loop(0, n, start, 0, unroll=DMA_UNROLL)
    lax.fori_loop(0, n, wait, 0, unroll=DMA_UNROLL)


def _dispatch_kernel(dest_ref, xn_ref, init_hbm, xb_hbm, sem):
    del init_hbm
    n = dest_ref.shape[2]
    _row_dma_loop(n, lambda j: pltpu.make_async_copy(
        xn_ref.at[pl.ds(j >> 1, 1)], xb_hbm.at[pl.ds(dest_ref[0, 0, j], 1)], sem))


def _dispatch(xn, dest, rows):
    N, D = xn.shape
    tm = dest.shape[2] // TOP_IN_GROUP
    return pl.pallas_call(
        _dispatch_kernel,
        grid=(N // tm,),
        in_specs=[pl.BlockSpec((1, 1, TOP_IN_GROUP * tm), lambda i: (i, 0, 0), memory_space=pltpu.SMEM),
                  pl.BlockSpec((tm, D), lambda i: (i, 0)), pl.BlockSpec(memory_space=pl.ANY)],
        out_specs=pl.BlockSpec(memory_space=pl.ANY),
        out_shape=jax.ShapeDtypeStruct((rows, D), F32),
        scratch_shapes=[pltpu.SemaphoreType.DMA(())],
        input_output_aliases={2: 0},
        compiler_params=_cparams("arbitrary"),
        name="moe_dispatch",
    )(dest, xn, jnp.zeros((rows, D), F32))


def _expert_kernel(be_ref, nact_ref, xb_ref, wu_ref, wd_ref, yb_ref):
    del be_ref
    active = pl.program_id(0) < nact_ref[0]

    @pl.when(active)
    def _():
        ab = _mm(xb_ref[...], wu_ref)
        a, b = ab[:, 0:D_EXPERT], ab[:, D_EXPERT:2 * D_EXPERT]
        yb_ref[...] = _mm(a * jax.nn.sigmoid(a) * b, wd_ref)

    @pl.when(jnp.logical_not(active))
    def _():
        yb_ref[...] = jnp.zeros(yb_ref.shape, F32)


def _experts(xb, blk_e, n_act, w_up, w_down):
    rows, D = xb.shape
    n_blk = rows // FFN_BLOCK
    P = w_up.shape[0]
    return pl.pallas_call(
        _expert_kernel,
        grid_spec=pltpu.PrefetchScalarGridSpec(
            num_scalar_prefetch=2,
            grid=(n_blk,),
            in_specs=[pl.BlockSpec((FFN_BLOCK, D), lambda i, be, na: (i, 0)),
                      pl.BlockSpec((P, None, D, 2 * D_EXPERT), lambda i, be, na: (0, be[i], 0, 0)),
                      pl.BlockSpec((P, None, D_EXPERT, D), lambda i, be, na: (0, be[i], 0, 0))],
            out_specs=pl.BlockSpec((FFN_BLOCK, D), lambda i, be, na: (i, 0)),
        ),
        out_shape=jax.ShapeDtypeStruct((rows, D), F32),
        compiler_params=_cparams("arbitrary"),
        name="moe_experts",
    )(blk_e, n_act, xb, w_up, w_down)


def _combine_kernel(dest_ref, x_ref, route_ref, yb_hbm, o_ref, y0_ref, y1_ref, sem):
    tm = x_ref.shape[0]
    for k, buf in enumerate((y0_ref, y1_ref)):
        _row_dma_loop(tm, lambda t, k=k, buf=buf: pltpu.make_async_copy(
            yb_hbm.at[pl.ds(dest_ref[0, 0, TOP_IN_GROUP * t + k], 1)], buf.at[pl.ds(t, 1)], sem.at[k]))
    route = route_ref[...]
    o_ref[...] = x_ref[...] + (route[:, 2:3] * y0_ref[...] + route[:, 3:4] * y1_ref[...])


def _combine(x1, route, yb, dest):
    N, D = x1.shape
    tm = dest.shape[2] // TOP_IN_GROUP
    return pl.pallas_call(
        _combine_kernel,
        grid=(N // tm,),
        in_specs=[pl.BlockSpec((1, 1, TOP_IN_GROUP * tm), lambda i: (i, 0, 0), memory_space=pltpu.SMEM),
                  pl.BlockSpec((tm, D), lambda i: (i, 0)), pl.BlockSpec((tm, LANES), lambda i: (i, 0)),
                  pl.BlockSpec(memory_space=pl.ANY)],
        out_specs=pl.BlockSpec((tm, D), lambda i: (i, 0)),
        out_shape=jax.ShapeDtypeStruct((N, D), F32),
        scratch_shapes=[pltpu.VMEM((tm, D), F32), pltpu.VMEM((tm, D), F32), pltpu.SemaphoreType.DMA((2,))],
        compiler_params=_cparams("arbitrary"),
        name="moe_combine",
    )(dest, x1, route, yb)


def _ffn(x1, xn, route, w):
    N, D = xn.shape
    M = N * TOP_IN_GROUP
    rank, cnt = _rank(route)
    counts = cnt[0, :N_EXPERTS].astype(jnp.int32)
    padded = (counts + FFN_BLOCK - 1) // FFN_BLOCK * FFN_BLOCK
    pend = jnp.cumsum(padded)
    pstart = pend - padded
    n_blk = -(-M // FFN_BLOCK) + N_EXPERTS
    e_id = route[:, 0:TOP_IN_GROUP].astype(jnp.int32)
    onehot = e_id[:, :, None] == jnp.arange(N_EXPERTS, dtype=jnp.int32)
    dest = jnp.sum(jnp.where(onehot, pstart, 0), axis=-1) + rank[:, 0:TOP_IN_GROUP].astype(jnp.int32)
    tm = min(ROW_TILE, N)
    dest = dest.reshape(N // tm, 1, TOP_IN_GROUP * tm)
    blk_start = jnp.arange(n_blk, dtype=jnp.int32) * FFN_BLOCK
    blk_e = jnp.minimum(jnp.sum(pend[None, :] <= blk_start[:, None], axis=1), N_EXPERTS - 1).astype(jnp.int32)
    n_act = (pend[-1] // FFN_BLOCK).astype(jnp.int32).reshape(1)
    xb = _dispatch(xn, dest, n_blk * FFN_BLOCK)
    yb = _experts(xb, blk_e, n_act, w['w_up'], w['w_down'])
    return _combine(x1, route, yb, dest)


def kernel(x_prompt, x_sample, cache_cmp_kv, cache_sel_kv, state_win_kv, state_conv, cache_mem_kv, page_table, mem_prompt, norm_mix, norm_ffn, w_in, q_gain, k_gain, w_cmp1, w_cmp2, pos_cmp, conv_w, conv_b, conv_ln_g, conv_ln_b, mem_norm, w_mem_kv, mem_qk_gain, w_branch, w_gate, b_gate, w_out, w_grp, b_grp, w_route, b_route, w_up, w_down):
    params = dict(norm_mix=norm_mix, norm_ffn=norm_ffn, w_in=w_in, q_gain=q_gain, k_gain=k_gain, w_cmp1=w_cmp1,
                  w_cmp2=w_cmp2, pos_cmp=pos_cmp, conv_w=conv_w, conv_b=conv_b, conv_ln_g=conv_ln_g,
                  conv_ln_b=conv_ln_b, mem_norm=mem_norm, w_mem_kv=w_mem_kv, mem_qk_gain=mem_qk_gain,
                  w_branch=w_branch, w_gate=w_gate, b_gate=b_gate, w_out=w_out, w_grp=w_grp, b_grp=b_grp,
                  w_route=w_route, b_route=b_route, w_up=w_up, w_down=w_down)
    B, T, D = x_prompt.shape
    DB, DS, _ = x_sample.shape
    depth = w_in.shape[0]
    R = T_SAMPLE_PAD
    xp = x_prompt.reshape(B * T, D)
    xs = jnp.pad(x_sample, ((0, 0), (0, R - DS), (0, 0))).reshape(DB * R, D)
    slot_minor = lambda a: jnp.transpose(a, (0, 1, 3, 4, 5, 2)).reshape(a.shape[0], a.shape[1], 256, a.shape[2])
    cmp_t, sel_t, win_t = slot_minor(cache_cmp_kv), slot_minor(cache_sel_kv), slot_minor(state_win_kv)
    outs = [[] for _ in range(9)]
    for l in range(depth):
        w = _layer_weights(l, params)
        xp, p_outs = _prompt_layer(xp, (B, T), mem_prompt, w)
        precise = l < depth - 1
        xs, s_outs = _sample_layer(l, xs, (DB, DS), cmp_t, sel_t, win_t, state_win_kv[l], state_conv[l],
                                   cache_mem_kv[l], page_table, _precise_weights(l, params, w) if precise else w, precise)
        for dst, o in zip(outs, p_outs + s_outs):
            dst.append(o)
    y_prompt = xp.reshape(B, T, D)
    y_sample = xs.reshape(DB, R, D)[:, :DS]
    return (y_prompt, y_sample) + tuple(jnp.stack(o) for o in outs)


def _kv5(a, nb, t):
    return a.reshape(nb, t, 2, N_KV, HEAD_DIM)


def _prompt_layer(xp, bt, mem_prompt, w):
    B, T = bt
    q, kvc, kvs, kvs_b, kvw, kvw_b, gates, u, qx = _proj_in(xp, w['nm'], w['w_in'], w['qg'], w['kg_sel'], w['kg_win'],
                                                            w['xq_gain'])
    n_pg = B * T // PAGE_SIZE
    fs = _cmp_matmul(kvc.reshape(n_pg, 8, 4096), jnp.arange(n_pg, dtype=jnp.int32), w['w1big'])
    kc, vc = _cmp_finish(fs, B, w['pos2'], w['w1f_hi'], w['w1f_lo'], w['w2big'], w['kg_cmp'])
    r3 = lambda a: a.reshape(B, T, a.shape[-1])
    o_nsa = _nsa_prompt(r3(q), r3(gates), kc, vc, r3(kvs_b), r3(kvw_b))
    o_conv = _conv_prompt(r3(u), w['conv_w'], w['conv_b'], w['ln_g'], w['ln_b'])
    m_kv = _mem_kv(mem_prompt, w['mem_norm'], w['w_mem'], w['xk_gain'])
    o_mem = _mem_attend(r3(qx), m_kv, 512)
    x1, xn, route = _tail(xp, o_nsa.reshape(B * T, 512), o_conv.reshape(B * T, 512), o_mem.reshape(B * T, 512),
                          w['nm'], w['wb'], w['wg'], w['bg'], w['wo'], w['nf'], w['wr_hi'], w['wr_lo'], w['br'])
    n_win = min(WINDOW, T)
    outs = [_kv5(kvc, B, T), _kv5(kvs, B, T), _kv5(kvw, B, T)[:, T - n_win:], r3(u)[:, T - (CONV_W - 1):],
            m_kv.reshape(B, -1, 2, X_HEADS, X_HEAD_DIM)]
    return _ffn(x1, xn, route, w), outs


def _sample_layer(l, xs, bs, cmp_t, sel_t, win_t, win_state, conv_state, mem_kv, page_table, w, precise=False):
    DB, DS = bs
    R = T_SAMPLE_PAD
    D = xs.shape[1]
    q, kvc, kvs, _, kvw, _, gates, u, qx = _proj_in(xs, w['nm'], w['w_in'], w['qg'], w['kg_sel'], w['kg_win'], w['xq_gain'])
    s3 = lambda a: a.reshape(DB, R, a.shape[-1])
    fs = _cmp_matmul_t(cmp_t, l, page_table.reshape(-1), w['w1big'])
    kc, vc = _cmp_finish(fs, DB, w['pos2'], w['w1f_hi'], w['w1f_lo'], w['w2big'], w['kg_cmp'])
    o_nsa = _nsa_sample(s3(q), s3(gates), kc, vc, s3(kvs), win_t, s3(kvw), sel_t, l, page_table, precise)
    u_ext = jnp.concatenate([conv_state, s3(u)[:, :DS]], axis=1)
    o_conv = _conv_sample(u_ext.transpose(1, 0, 2), w['conv_w'], w['conv_b'], w['ln_g'], w['ln_b'], DS)
    o_conv = jnp.pad(o_conv.transpose(1, 0, 2), ((0, 0), (0, R - DS), (0, 0))).reshape(DB * R, CONV_CH)
    o_mem = _mem_attend(s3(qx), mem_kv.reshape(DB, -1, 1024), R, precise)
    x1, xn, route = _tail(xs, o_nsa.reshape(DB * R, 512), o_conv, o_mem.reshape(DB * R, 512),
                          w['nm'], w['wb'], w['wg'], w['bg'], w['wo'], w['nf'], w['wr_hi'], w['wr_lo'], w['br'])
    real = lambda a: a.reshape(DB, R, -1)[:, :DS].reshape(DB * DS, -1)
    x2 = _ffn(real(x1), real(xn), real(route), w)
    xs = jnp.pad(x2.reshape(DB, DS, D), ((0, 0), (0, R - DS), (0, 0))).reshape(DB * R, D)
    win_all = jnp.concatenate([win_state, _kv5(kvw, DB, R)[:, :DS]], axis=1)
    outs = [_kv5(kvc, DB, R)[:, :DS], _kv5(kvs, DB, R)[:, :DS], win_all[:, DS:], u_ext[:, -(CONV_W - 1):]]
    return xs, outs
```

```python
import functools

import numpy as np
import jax
import jax.numpy as jnp
from jax import lax
from jax.experimental import pallas as pl
from jax.experimental.pallas import tpu as pltpu

F32 = jnp.float32
BF16 = jnp.bfloat16

D_MODEL = 1024
N_HEADS = 8
HEAD_DIM = 64
N_KV = 2
Q_PER_KV = 4
L_CMP = 32
S_CMP = 16
CMP_HIDDEN = 128
L_SEL = 64
N_SEL_TOP = 16
WINDOW = 512
Q_BLOCK = 128
FORCE_BONUS = 1.0e4
CONV_CH = 512
CONV_W = 31
X_HEADS = 4
X_HEAD_DIM = 128
N_GROUPS = 4
EXP_PER_GROUP = 8
N_EXPERTS = 32
TOP_IN_GROUP = 2
D_EXPERT = 512
PAGE_SIZE = 128
EPS = 1e-6

LANES = 128
NEG = -1e30
TINY = float(np.finfo(np.float32).tiny)
VMEM_LIMIT = 56 * 1024 * 1024

C_Q, C_KVC, C_KVS, C_KVW, C_GLU_A, C_GLU_B, C_QX, C_GATE, C_END = 0, 512, 768, 1024, 1280, 1792, 2304, 2816, 2944

TM_PROJ = 256
TM_TAIL = 256
SEL_CHUNK = 512
CMP_PAGES = 32
SAMPLE_PAGES = 16
T_SAMPLE_PAD = 8


def _cparams(*sem):
    return pltpu.CompilerParams(dimension_semantics=sem, vmem_limit_bytes=VMEM_LIMIT)


def _dot(a, b):
    return jnp.dot(a, b, preferred_element_type=F32)


def _dot_nt(a, b):
    return lax.dot_general(a, b, (((1,), (1,)), ((), ())), preferred_element_type=F32)


def _split(a):
    hi = a.astype(BF16)
    lo = (a - hi.astype(F32)).astype(BF16)
    return hi, lo


def _dot_hl(a, s):
    hi, lo = _split(a)
    return _dot(hi, s) + _dot(lo, s)


def _dot3(a, b, nt=False):
    dot = _dot_nt if nt else _dot
    ah, al = _split(a)
    bh, bl = _split(b)
    m = a.shape[0]
    top = dot(jnp.concatenate([ah, al], axis=0), bh)
    return top[:m] + top[m:] + dot(ah, bl)


def _mm(a, w_ref, rows=slice(None)):
    if w_ref.shape[0] == 1:
        return _dot(a.astype(BF16), w_ref[0, rows])
    ah, al = _split(a)
    m = a.shape[0]
    top = _dot(jnp.concatenate([ah, al], axis=0), w_ref[0, rows])
    return top[:m] + top[m:] + _dot(ah, w_ref[1, rows])


def _const_spec(shape):
    return pl.BlockSpec(shape, lambda *_: (0,) * len(shape), pipeline_mode=pl.Buffered(1))


def _rms(x, g):
    return x * lax.rsqrt(jnp.mean(x * x, axis=-1, keepdims=True) + EPS) * g


def _segnorm64(v, seg, segt, gain):
    ss = _dot_hl(v * v, seg)
    inv = lax.rsqrt(ss * (1.0 / HEAD_DIM) + EPS)
    return v * _dot_hl(inv, segt) * gain


def _masked_softmax(s, mask):
    sm = jnp.where(mask, s, NEG)
    m = jnp.max(sm, axis=-1, keepdims=True)
    e = jnp.where(mask, jnp.exp(sm - m), 0.0)
    return e / jnp.maximum(jnp.sum(e, axis=-1, keepdims=True), TINY)


def _iota(shape, dim):
    return lax.broadcasted_iota(jnp.int32, shape, dim)


def _seg_mats(width):
    n = width // HEAD_DIM
    s = np.zeros((width, LANES), np.float32)
    s[np.arange(width), np.arange(width) // HEAD_DIM] = 1.0
    assert n <= LANES
    return jnp.asarray(s, BF16), jnp.asarray(s.T.copy(), BF16)


def _proj_in_kernel(x_ref, nm_ref, w_ref, qg_ref, kgs_ref, kgw_ref, xg_ref, s8_ref, s8t_ref, s2_ref, s2t_ref,
                    q_ref, kvc_ref, kvs_ref, kvsb_ref, kvw_ref, kvwb_ref, gate_ref, u_ref, qx_ref):
    y = _mm(_rms(x_ref[...], nm_ref[...]), w_ref)
    q_ref[...] = _segnorm64(y[:, C_Q:C_KVC], s8_ref[...], s8t_ref[...], qg_ref[...])
    kvc_ref[...] = y[:, C_KVC:C_KVS]
    for c0, gain_ref, o_ref, ob_ref in ((C_KVS, kgs_ref, kvs_ref, kvsb_ref), (C_KVW, kgw_ref, kvw_ref, kvwb_ref)):
        kn = _segnorm64(y[:, c0:c0 + 128], s2_ref[...], s2t_ref[...], gain_ref[...])
        v = y[:, c0 + 128:c0 + 256]
        o_ref[:, 0:128] = kn
        o_ref[:, 128:256] = v
        ob_ref[:, 0:128] = kn.astype(BF16)
        ob_ref[:, 128:256] = v.astype(BF16)
    gate_ref[...] = jax.nn.sigmoid(y[:, C_GATE:C_END])
    u_ref[...] = y[:, C_GLU_A:C_GLU_B] * jax.nn.sigmoid(y[:, C_GLU_B:C_QX])
    xg = xg_ref[...]
    for h in range(X_HEADS):
        seg = y[:, C_QX + 128 * h:C_QX + 128 * (h + 1)]
        qx_ref[:, 128 * h:128 * (h + 1)] = _rms(seg, xg)


def _proj_in(x, nm, w, qg, kgs, kgw, xg):
    M = x.shape[0]
    tm = min(TM_PROJ, M)
    s8, s8t = _seg_mats(512)
    s2, s2t = _seg_mats(128)
    row = lambda i: (i, 0)
    fix = lambda i: (0, 0)
    widths = [(512, F32), (256, F32), (256, F32), (256, BF16), (256, F32), (256, BF16), (128, F32), (512, F32), (512, F32)]
    return pl.pallas_call(
        _proj_in_kernel,
        grid=(M // tm,),
        in_specs=[pl.BlockSpec((tm, D_MODEL), row), pl.BlockSpec((1, D_MODEL), fix),
                  _const_spec(w.shape), pl.BlockSpec((1, 512), fix), pl.BlockSpec((1, 128), fix),
                  pl.BlockSpec((1, 128), fix), pl.BlockSpec((1, 128), fix),
                  pl.BlockSpec((512, LANES), fix), pl.BlockSpec((LANES, 512), fix),
                  pl.BlockSpec((128, LANES), fix), pl.BlockSpec((LANES, 128), fix)],
        out_specs=[pl.BlockSpec((tm, wd), row) for wd, _ in widths],
        out_shape=[jax.ShapeDtypeStruct((M, wd), dt) for wd, dt in widths],
        compiler_params=_cparams("parallel"),
        name="proj_in",
    )(x, nm, w, qg, kgs, kgw, xg, s8, s8t, s2, s2t)


def _cmp_matmul_kernel(pt_ref, *refs):
    del pt_ref
    page_refs, w_ref, o_ref = refs[:-2], refs[-2], refs[-1]
    rows = jnp.concatenate([r[...] for r in page_refs], axis=0).astype(BF16)
    o_ref[...] = _dot(rows, w_ref[...])


def _cmp_matmul(pages, page_ids, w1big):
    n = page_ids.shape[0]
    P = CMP_PAGES
    page_spec = lambda k: pl.BlockSpec((None, 8, 4096), lambda i, pt: (pt[i * P + k], 0, 0))
    return pl.pallas_call(
        _cmp_matmul_kernel,
        grid_spec=pltpu.PrefetchScalarGridSpec(
            num_scalar_prefetch=1,
            grid=(n // P,),
            in_specs=[page_spec(k) for k in range(P)] + [pl.BlockSpec((4096, 1024), lambda i, pt: (0, 0))],
            out_specs=pl.BlockSpec((P * 8, 1024), lambda i, pt: (i, 0)),
        ),
        out_shape=jax.ShapeDtypeStruct((n * 8, 1024), F32),
        compiler_params=_cparams("arbitrary"),
        name="cmp_matmul",
    )(page_ids, *([pages] * P), w1big)


def _cmp_matmul_t_kernel(pt_ref, *refs):
    del pt_ref
    P = CMP_PAGES
    page_refs, w_ref, o_ref, rows_ref = refs[:P], refs[P], refs[P + 1], refs[P + 2]
    for k, r in enumerate(page_refs):
        rows = r[...].T
        for half in range(2):
            rows_ref[half, k * PAGE_SIZE:(k + 1) * PAGE_SIZE, :] = rows[:, half * LANES:(half + 1) * LANES]
    n_chunk = P * PAGE_SIZE // S_CMP
    outs = []
    for half in range(2):
        lhs = jnp.concatenate([rows_ref[half, pl.ds(p, n_chunk, stride=S_CMP), :] for p in range(S_CMP)], axis=1)
        outs.append(_mm(lhs, w_ref.at[half]))
    o_ref[...] = jnp.concatenate([outs[0][:, 0:256], outs[1][:, 0:256], outs[0][:, 256:512], outs[1][:, 256:512]], axis=1)


def _cmp_matmul_t(pages_t, layer, page_ids, w1kv):
    n = page_ids.shape[0]
    P = CMP_PAGES
    page_spec = lambda k: pl.BlockSpec((None, None, 256, PAGE_SIZE), lambda i, pt: (layer, pt[i * P + k], 0, 0))
    return pl.pallas_call(
        _cmp_matmul_t_kernel,
        grid_spec=pltpu.PrefetchScalarGridSpec(
            num_scalar_prefetch=1,
            grid=(n // P,),
            in_specs=[page_spec(k) for k in range(P)] + [_const_spec(w1kv.shape)],
            out_specs=pl.BlockSpec((P * 8, 1024), lambda i, pt: (i, 0)),
            scratch_shapes=[pltpu.VMEM((2, P * PAGE_SIZE, LANES), F32)],
        ),
        out_shape=jax.ShapeDtypeStruct((n * 8, 1024), F32),
        compiler_params=_cparams("arbitrary"),
        name="cmp_matmul_t",
    )(page_ids, *([pages_t] * P), w1kv)


def _gelu_tanh(x):
    return 0.5 * x * (1.0 + jnp.tanh(np.float32(np.sqrt(2.0 / np.pi)) * (x + 0.044715 * (x * x * x))))


def _cmp_finish_kernel(fs_ref, pos_ref, w1f_ref, w1fl_ref, w2_ref, kg_ref, s2_ref, s2t_ref, kc_ref, vc_ref):
    nc = fs_ref.shape[0]
    ph, plo = _split(pos_ref[...])
    pb = _dot(ph, w1f_ref[...]) + _dot(plo, w1f_ref[...]) + _dot(ph, w1fl_ref[...])
    lane = _iota((1, 512), 1)
    pos_bias = jnp.where(lane < 256, jnp.tile(pb[0:1], (1, 4)), jnp.tile(pb[1:2], (1, 4)))
    first = fs_ref[:, 0:512]
    second = pltpu.roll(fs_ref[:, 512:1024], nc - 1, 0)
    hid = _gelu_tanh(first + second + pos_bias)
    out = _mm(hid, w2_ref)
    kc_ref[...] = _segnorm64(out[:, 0:128], s2_ref[...], s2t_ref[...], kg_ref[...])
    vc_ref[...] = out[:, 128:256]


def _cmp_finish(fs, nb, pos2, w1flat, w1flat_lo, w2big, kg):
    nc = fs.shape[0] // nb
    s2, s2t = _seg_mats(128)
    fix = lambda b: (0, 0)
    return pl.pallas_call(
        _cmp_finish_kernel,
        grid=(nb,),
        in_specs=[pl.BlockSpec((nc, 1024), lambda b: (b, 0)), pl.BlockSpec((8, 4096), fix),
                  pl.BlockSpec((4096, 128), fix), pl.BlockSpec((4096, 128), fix), _const_spec(w2big.shape),
                  pl.BlockSpec((1, 128), fix), pl.BlockSpec((128, LANES), fix), pl.BlockSpec((LANES, 128), fix)],
        out_specs=[pl.BlockSpec((None, nc, 128), lambda b: (b, 0, 0))] * 2,
        out_shape=[jax.ShapeDtypeStruct((nb, nc, 128), F32)] * 2,
        compiler_params=_cparams("parallel"),
        name="cmp_finish",
    )(fs, pos2, w1flat, w1flat_lo, w2big, kg, s2, s2t)


def _qpad_group(q, g, dtype=BF16):
    lane = _iota((q.shape[0], LANES), 1)
    keep = (lane < HEAD_DIM) if g == 0 else (lane >= HEAD_DIM)
    parts = []
    for r in range(Q_PER_KV):
        h = Q_PER_KV * g + r
        x = q[:, 128 * (h // 2):128 * (h // 2 + 1)]
        if h % 2 != g:
            x = pltpu.roll(x, HEAD_DIM, 1)
        parts.append(jnp.where(keep, x * (HEAD_DIM ** -0.5), 0.0))
    return jnp.concatenate(parts, axis=0).astype(dtype)


def _slopes(rows_per_head, g):
    r = _iota((Q_PER_KV * rows_per_head, 1), 0) >> (rows_per_head.bit_length() - 1)
    out = jnp.zeros(r.shape, F32)
    for k in range(Q_PER_KV):
        out = jnp.where(r == k, np.float32(2.0 ** (-(Q_PER_KV * g + k + 1))), out)
    return out


def _tile4(x):
    return jnp.concatenate([x] * Q_PER_KV, axis=0)


def _top_select(score, n_top):
    lane = _iota(score.shape, 1).astype(F32)
    big = np.float32(score.shape[1])
    sel = jnp.zeros(score.shape, F32)
    sc = score
    for _ in range(n_top):
        m = jnp.max(sc, axis=-1, keepdims=True)
        idx = jnp.min(jnp.where(sc == m, lane, big), axis=-1, keepdims=True)
        pick = lane == idx
        sel = jnp.where(pick, 1.0, sel)
        sc = jnp.where(pick, -3.0e38, sc)
    return sel


def _select_blocks(p, cover, qpos, n_blk_real):
    R = qpos.shape[0]
    psum = p[0:R] + p[R:2 * R] + p[2 * R:3 * R] + p[3 * R:4 * R]
    imp = _dot_hl(psum, cover)
    blk = _iota(imp.shape, 1)
    cur = qpos >> 6
    valid = (blk * L_SEL <= qpos) & (blk < n_blk_real)
    forced = (blk == 0) | (blk == cur) | (blk == cur - 1)
    score = jnp.where(valid, imp + jnp.where(forced, FORCE_BONUS, 0.0), -FORCE_BONUS)
    score = jnp.where(blk < n_blk_real, score, -2.0e38)
    sel = _top_select(score, N_SEL_TOP)
    return jnp.where(valid, sel, 0.0)


def _merge_heads(gates, oc, os_, ow, R):
    lane = _iota((R, LANES), 1)
    tiles = []
    for t in range(N_HEADS // 2):
        g = t // 2
        halves = []
        for h in (2 * t, 2 * t + 1):
            r = h % Q_PER_KV
            sl = slice(r * R, (r + 1) * R)
            o = (gates[:, 3 * h:3 * h + 1] * oc[g][sl] + gates[:, 3 * h + 1:3 * h + 2] * os_[g][sl]
                 + gates[:, 3 * h + 2:3 * h + 3] * ow[g][sl])
            halves.append(o)
        lo, hi = halves
        if g == 0:
            hi = pltpu.roll(hi, HEAD_DIM, 1)
        else:
            lo = pltpu.roll(lo, HEAD_DIM, 1)
        tiles.append(jnp.where(lane < HEAD_DIM, lo, hi))
    return jnp.concatenate(tiles, axis=1)


MASK_BIAS = -1e30
X_HI, X_LO, X_CHUNK, X_ROW = 8, 9, 10, 11
X_ONE = X_CHUNK


def _ext_cols(lane4, cols):
    out = jnp.zeros(lane4.shape, F32)
    for k, v in cols:
        out = jnp.where(lane4 == k, v, out)
    return out


def _top_select_t(score, n_top):
    row = _iota(score.shape, 0).astype(F32)
    big = np.float32(score.shape[0])
    sel = jnp.zeros(score.shape, F32)
    sc = score
    for _ in range(n_top):
        m = jnp.max(sc, axis=0, keepdims=True)
        idx = jnp.min(jnp.where(sc == m, row, big), axis=0, keepdims=True)
        pick = row == idx
        sel = jnp.where(pick, 1.0, sel)
        sc = jnp.where(pick, -3.0e38, sc)
    return sel


def _nsa_prompt_kernel(q_ref, g_ref, kc_ref, vc_ref, kvs_ref, kvw_ref, covt_ref, xc_ref, xs_ref, xw_ref, o_ref):
    R = Q_BLOCK
    i = pl.program_id(1)
    start = i * R
    startf = start.astype(F32)
    q = q_ref[...]
    qloc = _iota((R, 1), 0)
    n_cmp = kc_ref.shape[0]
    nb = covt_ref.shape[0]
    last = (start + R - 1) // SEL_CHUNK
    k_last = pl.multiple_of(last * SEL_CHUNK, SEL_CHUNK)
    ws = pl.multiple_of(jnp.maximum(start - WINDOW, 0), R)
    woff = start - ws
    bias_c = jnp.where(start + qloc - (_iota((1, n_cmp), 1) * S_CMP + (L_CMP - 1)) >= 0, 0.0, MASK_BIAS)
    dw = woff + qloc - _iota((1, WINDOW + R), 1)
    bias_w = jnp.where((dw >= 0) & (dw < WINDOW), 0.0, MASK_BIAS)
    bias_d = jnp.where(start + qloc - (k_last + _iota((1, SEL_CHUNK), 1)) >= 0, 0.0, MASK_BIAS)
    has_cmp = jnp.where(start + qloc >= L_CMP - 1, 1.0, 0.0)
    qloc4 = _tile4(qloc.astype(F32))
    lane4 = _iota((Q_PER_KV * R, LANES), 1)
    blk = _iota((nb, 1), 0)
    qpos_row = start + _iota((1, R), 1)
    cur = qpos_row >> 6
    valid_t = blk * L_SEL <= qpos_row
    forced_t = (blk == 0) | (blk == cur) | (blk == cur - 1)
    HR = Q_PER_KV * R
    qg = jnp.concatenate([_qpad_group(q, 0), _qpad_group(q, 1)], axis=0)
    slope = jnp.concatenate([_slopes(R, 0), _slopes(R, 1)], axis=0)
    qloc8 = jnp.concatenate([qloc4, qloc4], axis=0)
    lane8 = jnp.concatenate([lane4, lane4], axis=0)

    ext = _ext_cols(lane8, [(0, slope * 1024.0), (1, slope * 16.0), (2, -slope * startf),
                            (3, -slope * (qloc8 - (L_CMP - 1.0)))])
    lhs = jnp.concatenate([qg, ext.astype(BF16)], axis=1)
    s = _dot_nt(lhs, jnp.concatenate([kc_ref[...].astype(BF16), xc_ref[...]], axis=1))
    ps, psums = [], []
    for h in range(N_HEADS):
        sr = s[h * R:(h + 1) * R] + bias_c
        e = jnp.exp(sr - jnp.max(sr, axis=-1, keepdims=True))
        p = e * (has_cmp / jnp.sum(e, axis=-1, keepdims=True))
        ps.append(p.astype(BF16))
        if h % Q_PER_KV == 0:
            psums.append(p)
        else:
            psums[-1] = psums[-1] + p
    oc = _dot(jnp.concatenate(ps, axis=0), vc_ref[...].astype(BF16))
    selb = []
    for g in range(N_KV):
        ph, plo = _split(psums[g])
        imp_t = _dot_nt(covt_ref[...], ph) + _dot_nt(covt_ref[...], plo)
        score = jnp.where(valid_t, imp_t + jnp.where(forced_t, FORCE_BONUS, 0.0), -FORCE_BONUS)
        sel_t = _top_select_t(score, N_SEL_TOP)
        selb.append(jnp.where(valid_t & (sel_t > 0.5), 0.0, MASK_BIAS).T)

    kvw = kvw_ref[pl.ds(ws, WINDOW + R), :]
    ext = _ext_cols(lane8, [(X_HI, slope * 256.0), (X_LO, slope), (X_CHUNK, -slope * woff.astype(F32)),
                            (X_ROW, -slope * qloc8)])
    lhs = jnp.concatenate([qg, ext.astype(BF16)], axis=1)
    s = _dot_nt(lhs, jnp.concatenate([kvw[:, 0:128], xw_ref[...]], axis=1))
    es = []
    for h in range(N_HEADS):
        sr = s[h * R:(h + 1) * R] + bias_w
        es.append(jnp.exp(sr - jnp.max(sr, axis=-1, keepdims=True)).astype(BF16))
    acc = _dot(jnp.concatenate(es, axis=0), jnp.concatenate([kvw[:, 128:256], xw_ref[...]], axis=1))
    ow = acc[:, 0:LANES] / acc[:, LANES + X_ONE:LANES + X_ONE + 1]

    ext_base = _ext_cols(lane8, [(X_HI, slope * 256.0), (X_LO, slope), (X_ROW, -slope * qloc8)])

    def scores(c):
        k0 = pl.multiple_of(c * SEL_CHUNK, SEL_CHUNK)
        shift = lax.rem(nb - 8 * c, nb)
        sel8 = jnp.concatenate([_tile4(pltpu.roll(selb[0], shift, 1)), _tile4(pltpu.roll(selb[1], shift, 1))], axis=0)
        ext_l = jnp.where(lane8 < 8, sel8,
                          jnp.where(lane8 == X_CHUNK, slope * (k0 - start).astype(F32), ext_base))
        lhs = jnp.concatenate([qg, ext_l.astype(BF16)], axis=1)
        return _dot_nt(lhs, jnp.concatenate([kvs_ref[pl.ds(k0, SEL_CHUNK), 0:128], xs_ref[...]], axis=1))

    def absorb(c, sc, m, acc, diag):
        k0 = pl.multiple_of(c * SEL_CHUNK, SEL_CHUNK)
        es, ms = [], []
        for h in range(N_HEADS):
            rows = slice(h * R, (h + 1) * R)
            sr = sc[rows]
            if diag:
                sr = sr + bias_d
            m_new = jnp.maximum(m[rows], jnp.max(sr, axis=-1, keepdims=True))
            es.append(jnp.exp(sr - m_new).astype(BF16))
            ms.append(m_new)
        m_new = jnp.concatenate(ms, axis=0)
        v_aug = jnp.concatenate([kvs_ref[pl.ds(k0, SEL_CHUNK), 128:256], xs_ref[...]], axis=1)
        return m_new, jnp.exp(m - m_new) * acc + _dot(jnp.concatenate(es, axis=0), v_aug)

    def body(c, carry):
        sc, m, acc = carry
        sc_next = scores(c + 1)
        m, acc = absorb(c, sc, m, acc, False)
        return sc_next, m, acc

    init = (scores(0), jnp.full((2 * HR, 1), MASK_BIAS, F32), jnp.zeros((2 * HR, 2 * LANES), F32))
    sc, m, acc = lax.fori_loop(0, last, body, init)
    _, acc = absorb(last, sc, m, acc, True)
    os_ = acc[:, 0:LANES] / acc[:, LANES + X_ONE:LANES + X_ONE + 1]
    split = lambda a: [a[0:HR], a[HR:2 * HR]]
    o_ref[...] = _merge_heads(g_ref[...], split(oc), split(os_), split(ow), R)


def _pos_ext(n, sel_shift=None):
    j = np.arange(n)
    x = np.zeros((n, LANES), np.float32)
    if sel_shift is not None:
        x[j, (j >> sel_shift) & 7] = 1.0
    x[:, X_HI] = j >> 8
    x[:, X_LO] = j & 255
    x[:, X_CHUNK] = 1.0
    x[:, X_ROW] = 1.0
    return jnp.asarray(x, BF16)


def _nsa_prompt(q, gates, kc, vc, kvs_b, kvw_b):
    B, T, _ = q.shape
    n_cmp = kc.shape[1]
    n_sel = T // L_SEL
    assert n_sel <= LANES and T % SEL_CHUNK == 0
    start = np.arange(n_cmp)[None, :] * S_CMP
    j = np.arange(LANES)[:, None]
    cover_t = ((start < (j + 1) * L_SEL) & (start + L_CMP > j * L_SEL) & (np.arange(n_cmp)[None, :] < T // S_CMP - 1)
               & (j < n_sel))
    n = np.arange(n_cmp)
    xc = np.zeros((n_cmp, LANES), np.float32)
    xc[:, 0], xc[:, 1], xc[:, 2], xc[:, 3] = n >> 6, n & 63, 1.0, 1.0
    blk = lambda b, i: (b, i, 0)
    per_b = lambda b, i: (b, 0, 0)
    fix = lambda b, i: (0, 0)
    return pl.pallas_call(
        _nsa_prompt_kernel,
        grid=(B, T // Q_BLOCK),
        in_specs=[pl.BlockSpec((None, Q_BLOCK, 512), blk), pl.BlockSpec((None, Q_BLOCK, 128), blk),
                  pl.BlockSpec((None, n_cmp, 128), per_b), pl.BlockSpec((None, n_cmp, 128), per_b),
                  pl.BlockSpec((None, T, 256), per_b), pl.BlockSpec((None, T, 256), per_b),
                  pl.BlockSpec((LANES, n_cmp), fix), pl.BlockSpec((n_cmp, LANES), fix),
                  pl.BlockSpec((SEL_CHUNK, LANES), fix), pl.BlockSpec((WINDOW + Q_BLOCK, LANES), fix)],
        out_specs=pl.BlockSpec((None, Q_BLOCK, 512), blk),
        out_shape=jax.ShapeDtypeStruct((B, T, 512), F32),
        compiler_params=_cparams("parallel", "arbitrary"),
        name="nsa_prompt",
    )(q, gates, kc, vc, kvs_b, kvw_b, jnp.asarray(cover_t, BF16), jnp.asarray(xc, BF16),
      _pos_ext(SEL_CHUNK, 6), _pos_ext(WINDOW + Q_BLOCK))


def _nsa_sample_kernel(pt_ref, q_ref, g_ref, kc_ref, vc_ref, cov_ref, kvn_ref, win_ref, kwn_ref, *rest,
                       past, n_blk_real, precise):
    del pt_ref
    cast = (lambda a: a) if precise else (lambda a: a.astype(BF16))
    mm = _dot3 if precise else _dot
    mm_nt = functools.partial(_dot3, nt=True) if precise else _dot_nt
    P = SAMPLE_PAGES
    page_refs = rest[:P]
    o_ref = rest[P]
    qg_s, selm_s, m_s, l_s, acc_s, oc_s, ow_s = rest[P + 1:]
    R = T_SAMPLE_PAD
    HR = Q_PER_KV * R
    j = pl.program_id(1)
    nj = pl.num_programs(1)
    qpos = past + _iota((R, 1), 0)
    qpos8 = jnp.concatenate([_tile4(qpos)] * N_KV, axis=0)
    slope = jnp.concatenate([_slopes(R, g) for g in range(N_KV)], axis=0)
    n_cmp = kc_ref.shape[0]
    nb = cov_ref.shape[1]
    chunk = P * PAGE_SIZE
    zpad = jnp.zeros((LANES - R, 256), F32)

    @pl.when(j == 0)
    def _():
        q = q_ref[...]
        qg = jnp.concatenate([_qpad_group(q, g, qg_s.dtype) for g in range(N_KV)], axis=0)
        qg_s[...] = qg
        ncol = _iota((1, n_cmp), 1)
        d = qpos8 - (ncol * S_CMP + (L_CMP - 1))
        s = mm_nt(qg, cast(kc_ref[...])) - slope * d.astype(F32)
        p = _masked_softmax(s, (d >= 0) & (ncol < n_cmp - 1))
        oc_s[...] = mm(cast(p), cast(vc_ref[...]))
        for g in range(N_KV):
            selm_s[g] = _select_blocks(p[g * HR:(g + 1) * HR], cov_ref[...], qpos, n_blk_real)
        win = cast(win_ref[...])
        col = _iota((1, WINDOW), 1)
        d_old = qpos8 - (past - WINDOW + col)
        s_old = jnp.where(d_old < WINDOW, mm(qg, win[0:128]) - slope * d_old.astype(F32), NEG)
        kv_new = cast(jnp.concatenate([kwn_ref[...], zpad], axis=0))
        coln = _iota((1, LANES), 1)
        d_new = qpos8 - (past + coln)
        s_new = jnp.where((d_new >= 0) & (coln < 4), mm_nt(qg, kv_new[:, 0:128]) - slope * d_new.astype(F32), NEG)
        m = jnp.maximum(jnp.max(s_old, axis=-1, keepdims=True), jnp.max(s_new, axis=-1, keepdims=True))
        e_old = jnp.where(d_old < WINDOW, jnp.exp(s_old - m), 0.0)
        e_new = jnp.where((d_new >= 0) & (coln < 4), jnp.exp(s_new - m), 0.0)
        den = jnp.sum(e_old, axis=-1, keepdims=True) + jnp.sum(e_new, axis=-1, keepdims=True)
        num = mm_nt(cast(e_old), win[128:256]) + mm(cast(e_new), kv_new[:, 128:256])
        ow_s[...] = num / jnp.maximum(den, TINY)
        m_s[...] = jnp.full(m_s.shape, NEG, F32)
        l_s[...] = jnp.zeros(l_s.shape, F32)
        acc_s[...] = jnp.zeros(acc_s.shape, F32)

    def update(scores, values, kpos, extra_mask):
        nk = kpos.shape[1]
        expand = jnp.where(_iota((nb, nk), 0) == (kpos >> 6), 1.0, 0.0).astype(BF16)
        picked = jnp.concatenate([_tile4(_dot(selm_s[g].astype(BF16), expand)) for g in range(N_KV)], axis=0)
        dd = qpos8 - kpos
        mask = (picked > 0.5) & (dd >= 0)
        if extra_mask is not None:
            mask = mask & extra_mask
        sc = jnp.where(mask, scores(qg_s[...]) - slope * dd.astype(F32), NEG)
        m = m_s[...]
        m_new = jnp.maximum(m, jnp.max(sc, axis=-1, keepdims=True))
        a = jnp.exp(m - m_new)
        e = jnp.where(mask, jnp.exp(sc - m_new), 0.0)
        m_s[...] = m_new
        l_s[...] = a * l_s[...] + jnp.sum(e, axis=-1, keepdims=True)
        acc_s[...] = a * acc_s[...] + values(cast(e))

    k_t = jnp.concatenate([cast(r[0:128, :]) for r in page_refs], axis=1)
    v_t = jnp.concatenate([cast(r[128:256, :]) for r in page_refs], axis=1)
    update(lambda qg: mm(qg, k_t), lambda e: mm_nt(e, v_t), j * chunk + _iota((1, chunk), 1), None)

    @pl.when(j == nj - 1)
    def _():
        kv_new = cast(jnp.concatenate([kvn_ref[...], zpad], axis=0))
        col = _iota((1, LANES), 1)
        update(lambda qg: mm_nt(qg, kv_new[:, 0:128]), lambda e: mm(e, kv_new[:, 128:256]), past + col, col < 4)
        os_ = acc_s[...] / jnp.maximum(l_s[...], TINY)
        split = lambda a: [a[0:HR], a[HR:2 * HR]]
        o_ref[...] = _merge_heads(g_ref[...], split(oc_s[...]), split(os_), split(ow_s[...]), R)


def _nsa_sample(q, gates, kc, vc, kvs_new, win_t, kvw_new, sel_pages_t, layer, page_table, precise=False):
    DB = q.shape[0]
    n_pages = page_table.shape[1]
    past = n_pages * PAGE_SIZE
    P = SAMPLE_PAGES
    R = T_SAMPLE_PAD
    n_cmp = kc.shape[1]
    n_blk_real = -(-(past + 4) // L_SEL)
    nb = -(-n_blk_real // LANES) * LANES
    start = np.arange(n_cmp)[:, None] * S_CMP
    jj = np.arange(nb)[None, :]
    cover = ((start < (jj + 1) * L_SEL) & (start + L_CMP > jj * L_SEL) & (jj < n_blk_real)
             & (np.arange(n_cmp)[:, None] < n_cmp - 1))
    per_b = lambda b, j, pt: (b, 0, 0)
    page_spec = lambda k: pl.BlockSpec((None, None, 256, PAGE_SIZE),
                                       lambda b, j, pt: (layer, pt[b * n_pages + j * P + k], 0, 0))
    kern = functools.partial(_nsa_sample_kernel, past=past, n_blk_real=n_blk_real, precise=precise)
    rows = N_KV * Q_PER_KV * R
    return pl.pallas_call(
        kern,
        grid_spec=pltpu.PrefetchScalarGridSpec(
            num_scalar_prefetch=1,
            grid=(DB, n_pages // P),
            in_specs=[pl.BlockSpec((None, R, 512), per_b), pl.BlockSpec((None, R, 128), per_b),
                      pl.BlockSpec((None, n_cmp, 128), per_b), pl.BlockSpec((None, n_cmp, 128), per_b),
                      pl.BlockSpec((n_cmp, nb), lambda b, j, pt: (0, 0)),
                      pl.BlockSpec((None, R, 256), per_b),
                      pl.BlockSpec((None, None, 256, WINDOW), lambda b, j, pt: (layer, b, 0, 0)),
                      pl.BlockSpec((None, R, 256), per_b)] + [page_spec(k) for k in range(P)],
            out_specs=pl.BlockSpec((None, R, 512), per_b),
            scratch_shapes=[pltpu.VMEM((rows, LANES), F32 if precise else BF16), pltpu.VMEM((N_KV, R, nb), F32),
                            pltpu.VMEM((rows, 1), F32), pltpu.VMEM((rows, 1), F32),
                            pltpu.VMEM((rows, LANES), F32), pltpu.VMEM((rows, LANES), F32),
                            pltpu.VMEM((rows, LANES), F32)],
        ),
        out_shape=jax.ShapeDtypeStruct((DB, R, 512), F32),
        compiler_params=_cparams("parallel", "arbitrary"),
        name="nsa_sample",
    )(page_table.reshape(-1), q, gates, kc, vc, jnp.asarray(cover, BF16), kvs_new, win_t, kvw_new,
      *([sel_pages_t] * P))


def _ln_silu(c, g, b):
    mu = jnp.mean(c, axis=-1, keepdims=True)
    xc = c - mu
    y = xc * lax.rsqrt(jnp.mean(xc * xc, axis=-1, keepdims=True) + EPS) * g + b
    return y * jax.nn.sigmoid(y)


def _conv_prompt_kernel(um_ref, up_ref, w_ref, b_ref, g_ref, bb_ref, o_ref, ext_ref):
    tt = um_ref.shape[0]
    i = pl.program_id(1)
    ext_ref[0:32, :] = jnp.where(i > 0, up_ref[...], 0.0)
    ext_ref[32:32 + tt, :] = um_ref[...]
    acc = jnp.zeros((tt, CONV_CH), F32)
    for j in range(CONV_W):
        acc = acc + ext_ref[pl.ds(j + 2, tt), :] * w_ref[j:j + 1, :]
    o_ref[...] = _ln_silu(acc + b_ref[...], g_ref[...], bb_ref[...])


def _conv_prompt(u, w, b, g, bb):
    B, T, C = u.shape
    tt = 512
    fix = lambda bi, i: (0, 0)
    return pl.pallas_call(
        _conv_prompt_kernel,
        grid=(B, T // tt),
        in_specs=[pl.BlockSpec((None, tt, C), lambda bi, i: (bi, i, 0)),
                  pl.BlockSpec((None, 32, C), lambda bi, i: (bi, jnp.maximum(i * (tt // 32) - 1, 0), 0)),
                  pl.BlockSpec((32, C), fix), pl.BlockSpec((1, C), fix), pl.BlockSpec((1, C), fix),
                  pl.BlockSpec((1, C), fix)],
        out_specs=pl.BlockSpec((None, tt, C), lambda bi, i: (bi, i, 0)),
        out_shape=jax.ShapeDtypeStruct((B, T, C), F32),
        scratch_shapes=[pltpu.VMEM((tt + 32, C), F32)],
        compiler_params=_cparams("parallel", "arbitrary"),
        name="conv_prompt",
    )(u, u, w, b, g, bb)


def _conv_sample_kernel(ext_ref, w_ref, b_ref, g_ref, bb_ref, o_ref):
    n_out = o_ref.shape[0]
    for t in range(n_out):
        acc = jnp.zeros(o_ref.shape[1:], F32)
        for j in range(CONV_W):
            acc = acc + ext_ref[t + j] * w_ref[j:j + 1, :]
        o_ref[t] = _ln_silu(acc + b_ref[...], g_ref[...], bb_ref[...])


def _conv_sample(ext_t, w, b, g, bb, n_out):
    _, DB, C = ext_t.shape
    return pl.pallas_call(
        _conv_sample_kernel,
        out_shape=jax.ShapeDtypeStruct((n_out, DB, C), F32),
        name="conv_sample",
    )(ext_t, w, b, g, bb)


def _mem_kv_kernel(m_ref, g_ref, w_ref, kg_ref, o_ref):
    hb = _rms(m_ref[...], g_ref[...]).astype(BF16)
    kv = _dot(hb, w_ref[...])
    kg = kg_ref[...]
    for h in range(X_HEADS):
        o_ref[:, 128 * h:128 * (h + 1)] = _rms(kv[:, 128 * h:128 * (h + 1)], kg)
    o_ref[:, 512:1024] = kv[:, 512:1024]


def _mem_kv(mem, g, w, kg):
    B, M, D = mem.shape
    fix = lambda b: (0, 0)
    return pl.pallas_call(
        _mem_kv_kernel,
        grid=(B,),
        in_specs=[pl.BlockSpec((None, M, D), lambda b: (b, 0, 0)), pl.BlockSpec((1, D), fix),
                  pl.BlockSpec((D, 1024), fix), pl.BlockSpec((1, 128), fix)],
        out_specs=pl.BlockSpec((None, M, 1024), lambda b: (b, 0, 0)),
        out_shape=jax.ShapeDtypeStruct((B, M, 1024), F32),
        compiler_params=_cparams("parallel"),
        name="mem_kv",
    )(mem, g, w, kg)


def _mem_attend_kernel(q_ref, kv_ref, o_ref, *, precise):
    for h in range(X_HEADS):
        q = q_ref[:, 128 * h:128 * (h + 1)]
        k = kv_ref[:, 128 * h:128 * (h + 1)]
        v = kv_ref[:, 512 + 128 * h:512 + 128 * (h + 1)]
        s = (_dot3(q, k, nt=True) if precise else _dot_nt(q.astype(BF16), k.astype(BF16))) * (X_HEAD_DIM ** -0.5)
        e = jnp.exp(s - jnp.max(s, axis=-1, keepdims=True))
        p = e / jnp.sum(e, axis=-1, keepdims=True)
        o_ref[:, 128 * h:128 * (h + 1)] = _dot3(p, v) if precise else _dot(p.astype(BF16), v.astype(BF16))


def _mem_attend(qx, mkv, tq, precise=False):
    B, T, _ = qx.shape
    M = mkv.shape[1]
    return pl.pallas_call(
        functools.partial(_mem_attend_kernel, precise=precise),
        grid=(B, T // tq),
        in_specs=[pl.BlockSpec((None, tq, 512), lambda b, i: (b, i, 0)),
                  pl.BlockSpec((None, M, 1024), lambda b, i: (b, 0, 0))],
        out_specs=pl.BlockSpec((None, tq, 512), lambda b, i: (b, i, 0)),
        out_shape=jax.ShapeDtypeStruct((B, T, 512), F32),
        compiler_params=_cparams("parallel", "arbitrary"),
        name="mem_attend",
    )(qx, mkv)


def _tail_kernel(x_ref, on_ref, ocv_ref, om_ref, nm_ref, wb_ref, wg_ref, bg_ref, wo_ref, nf_ref,
                 wrh_ref, wrl_ref, br_ref, x1_ref, xn_ref, route_ref):
    x = x_ref[...]
    gate = jax.nn.sigmoid(_mm(_rms(x, nm_ref[...]), wg_ref) + bg_ref[...])
    y_nsa = _mm(on_ref[...], wb_ref, slice(0, 512))
    y_conv = _mm(ocv_ref[...], wb_ref, slice(512, 1024))
    y_mem = _mm(om_ref[...], wb_ref, slice(1024, 1536))
    merged = gate[:, 0:1024] * y_nsa + gate[:, 1024:2048] * y_conv + gate[:, 2048:3072] * y_mem
    x1 = x + _mm(merged, wo_ref)
    x1_ref[...] = x1
    xn = _rms(x1, nf_ref[...])
    xn_ref[...] = xn
    xh, xl = _split(xn)
    logits = _dot(xh, wrh_ref[...]) + _dot(xl, wrh_ref[...]) + _dot(xh, wrl_ref[...]) + br_ref[...]
    lane = _iota(logits.shape, 1)
    lanef = lane.astype(F32)
    is_g = lane < N_GROUPS
    gl = jnp.where(is_g, logits, NEG)
    gm = jnp.max(gl, axis=-1, keepdims=True)
    p_g = 1.0 / jnp.sum(jnp.where(is_g, jnp.exp(gl - gm), 0.0), axis=-1, keepdims=True)
    g_sel = jnp.min(jnp.where(gl == gm, lanef, 128.0), axis=-1, keepdims=True).astype(jnp.int32)
    lo = N_GROUPS + EXP_PER_GROUP * g_sel
    el = jnp.where((lane >= lo) & (lane < lo + EXP_PER_GROUP), logits, NEG)
    m0 = jnp.max(el, axis=-1, keepdims=True)
    i0 = jnp.min(jnp.where(el == m0, lanef, 128.0), axis=-1, keepdims=True)
    el = jnp.where(lanef == i0, 2.0 * NEG, el)
    m1 = jnp.max(el, axis=-1, keepdims=True)
    i1 = jnp.min(jnp.where(el == m1, lanef, 128.0), axis=-1, keepdims=True)
    t = jnp.exp(m1 - m0)
    w0 = p_g / (1.0 + t)
    w1 = p_g * t / (1.0 + t)
    route_ref[...] = jnp.where(lane == 0, i0 - N_GROUPS, jnp.where(lane == 1, i1 - N_GROUPS,
                               jnp.where(lane == 2, w0, jnp.where(lane == 3, w1, 0.0))))


def _tail(x, o_nsa, o_conv, o_mem, nm, wb, wg, bg, wo, nf, wrh, wrl, br):
    M = x.shape[0]
    tm = min(TM_TAIL, M)
    row = lambda i: (i, 0)
    fix = lambda i: (0, 0)
    return pl.pallas_call(
        _tail_kernel,
        grid=(M // tm,),
        in_specs=[pl.BlockSpec((tm, D_MODEL), row), pl.BlockSpec((tm, 512), row), pl.BlockSpec((tm, 512), row),
                  pl.BlockSpec((tm, 512), row), pl.BlockSpec((1, D_MODEL), fix), _const_spec(wb.shape),
                  _const_spec(wg.shape), pl.BlockSpec((1, 3072), fix), _const_spec(wo.shape),
                  pl.BlockSpec((1, D_MODEL), fix), pl.BlockSpec((D_MODEL, LANES), fix), pl.BlockSpec((D_MODEL, LANES), fix),
                  pl.BlockSpec((1, LANES), fix)],
        out_specs=[pl.BlockSpec((tm, D_MODEL), row), pl.BlockSpec((tm, D_MODEL), row), pl.BlockSpec((tm, LANES), row)],
        out_shape=[jax.ShapeDtypeStruct((M, D_MODEL), F32), jax.ShapeDtypeStruct((M, D_MODEL), F32),
                   jax.ShapeDtypeStruct((M, LANES), F32)],
        compiler_params=_cparams("parallel"),
        name="tail",
    )(x, o_nsa, o_conv, o_mem, nm, wb, wg, bg, wo, nf, wrh, wrl, br)


def _stack(w, precise):
    hi = w.astype(BF16)
    if not precise:
        return hi[None]
    return jnp.stack([hi, (w - hi.astype(F32)).astype(BF16)])


def _cmp_weights_t(w1, precise):
    eye = jnp.eye(N_KV, dtype=F32)
    halves = []
    for kv in range(2):
        w = w1[kv].reshape(2, S_CMP, HEAD_DIM, CMP_HIDDEN)
        big = jnp.einsum('spdf,gG->pgdsGf', w, eye).reshape(S_CMP * N_KV * HEAD_DIM, 2 * N_KV * CMP_HIDDEN)
        halves.append(_stack(big, precise))
    return jnp.stack(halves)


def _precise_weights(l, p, w):
    out = dict(w)
    out.update(w_in=_stack(w['w_in_f32'], True), wb=_stack(p['w_branch'][l], True), wg=_stack(p['w_gate'][l], True),
               wo=_stack(p['w_out'][l], True), w_up=_stack(p['w_up'][l], True), w_down=_stack(p['w_down'][l], True),
               w1kv=_cmp_weights_t(p['w_cmp1'][l], True), w2big=_stack(w['w2big_f32'], True))
    return out


def _layer_weights(l, p):
    w = p['w_in'][l]
    n_gate = 3 * N_HEADS
    w_re = jnp.concatenate([w[:, :1280], w[:, 1280 + n_gate:], w[:, 1280:1280 + n_gate],
                            jnp.zeros((D_MODEL, C_END - C_GATE - n_gate), F32)], axis=1)
    w1 = p['w_cmp1'][l]
    eye = jnp.eye(2, dtype=F32)
    halves = []
    for half in range(2):
        wh = w1[:, S_CMP * half:S_CMP * (half + 1)]
        big = jnp.einsum('kpdf,kK,gG->pkgdKGf', wh, eye, eye)
        halves.append(big.reshape(S_CMP * 256, 4 * CMP_HIDDEN))
    w1big = jnp.concatenate(halves, axis=1).astype(BF16)
    w1flat = jnp.concatenate([w1[0].reshape(L_CMP * HEAD_DIM, CMP_HIDDEN), w1[1].reshape(L_CMP * HEAD_DIM, CMP_HIDDEN)], axis=0)
    pos = p['pos_cmp'][l].reshape(2, L_CMP * HEAD_DIM)
    pos2 = jnp.zeros((8, 4096), F32).at[0, :2048].set(pos[0]).at[1, 2048:].set(pos[1])
    w1f_hi = w1flat.astype(BF16)
    w1f_lo = (w1flat - w1f_hi.astype(F32)).astype(BF16)
    w2 = p['w_cmp2'][l]
    w2big_f32 = jnp.einsum('kfd,kK,gG->kgfKGd', w2, eye, eye).reshape(4 * CMP_HIDDEN, 256)
    w2big = _stack(w2big_f32, False)
    wr = jnp.concatenate([p['w_grp'][l], p['w_route'][l], jnp.zeros((D_MODEL, LANES - N_GROUPS - N_EXPERTS), F32)], axis=1)
    wr_hi = wr.astype(BF16)
    wr_lo = (wr - wr_hi.astype(F32)).astype(BF16)
    br = jnp.concatenate([p['b_grp'][l], p['b_route'][l], jnp.zeros((LANES - N_GROUPS - N_EXPERTS,), F32)]).reshape(1, LANES)
    cw = jnp.concatenate([p['conv_w'][l], jnp.zeros((1, CONV_CH), F32)], axis=0)
    r1 = lambda a: a.reshape(1, -1)
    return dict(
        nm=r1(p['norm_mix'][l]), w_in=_stack(w_re, False), w_in_f32=w_re, qg=r1(jnp.tile(p['q_gain'][l], N_HEADS)),
        kg_cmp=r1(jnp.tile(p['k_gain'][l, 0], 2)), kg_sel=r1(jnp.tile(p['k_gain'][l, 1], 2)),
        kg_win=r1(jnp.tile(p['k_gain'][l, 2], 2)), xq_gain=r1(p['mem_qk_gain'][l, 0]), xk_gain=r1(p['mem_qk_gain'][l, 1]),
        w1big=w1big, w1kv=_cmp_weights_t(w1, False), pos2=pos2, w1f_hi=w1f_hi, w1f_lo=w1f_lo, w2big=w2big,
        w2big_f32=w2big_f32,
        conv_w=cw, conv_b=r1(p['conv_b'][l]), ln_g=r1(p['conv_ln_g'][l]), ln_b=r1(p['conv_ln_b'][l]),
        mem_norm=r1(p['mem_norm'][l]), w_mem=p['w_mem_kv'][l].astype(BF16),
        wb=_stack(p['w_branch'][l], False), wg=_stack(p['w_gate'][l], False), bg=r1(p['b_gate'][l]),
        wo=_stack(p['w_out'][l], False), nf=r1(p['norm_ffn'][l]), wr_hi=wr_hi, wr_lo=wr_lo, br=br,
        w_up=_stack(p['w_up'][l], False), w_down=_stack(p['w_down'][l], False),
    )


FFN_BLOCK = 256
ROW_TILE = 256
DMA_UNROLL = 8


def _rank_kernel(route_ref, tri_ref, rank_ref, cnt_ref, base_ref):
    @pl.when(pl.program_id(0) == 0)
    def _():
        base_ref[...] = jnp.zeros(base_ref.shape, F32)

    route = route_ref[...]
    lane = _iota(route.shape, 1)
    lanef = lane.astype(F32)
    o0 = jnp.where(lanef == route[:, 0:1], 1.0, 0.0)
    o1 = jnp.where(lanef == route[:, 1:2], 1.0, 0.0)
    both = o0 + o1
    before = _dot(tri_ref[...], both.astype(BF16)) + base_ref[0:1, :]
    r0 = jnp.sum(o0 * before, axis=-1, keepdims=True)
    r1 = jnp.sum(o1 * (before + o0), axis=-1, keepdims=True)
    rank_ref[...] = jnp.where(lane == 0, r0, jnp.where(lane == 1, r1, 0.0))
    total = base_ref[0:1, :] + jnp.sum(both, axis=0, keepdims=True)
    base_ref[...] = jnp.broadcast_to(total, base_ref.shape)
    cnt_ref[...] = jnp.broadcast_to(total, cnt_ref.shape)


def _rank(route):
    N = route.shape[0]
    tr = min(512, N)
    tri = jnp.asarray(np.tril(np.ones((tr, tr), np.float32), -1), BF16)
    return pl.pallas_call(
        _rank_kernel,
        grid=(N // tr,),
        in_specs=[pl.BlockSpec((tr, LANES), lambda i: (i, 0)), pl.BlockSpec((tr, tr), lambda i: (0, 0))],
        out_specs=[pl.BlockSpec((tr, LANES), lambda i: (i, 0)), pl.BlockSpec((8, LANES), lambda i: (0, 0))],
        out_shape=[jax.ShapeDtypeStruct((N, LANES), F32), jax.ShapeDtypeStruct((8, LANES), F32)],
        scratch_shapes=[pltpu.VMEM((8, LANES), F32)],
        compiler_params=_cparams("arbitrary"),
        name="moe_rank",
    )(route, tri)


def _row_dma_loop(n, make_copy):
    def start(j, c):
        make_copy(j).start()
        return c

    def wait(j, c):
        make_copy(j).wait()
        return c

    lax.fori_loop(0, n, start, 0, unroll=DMA_UNROLL)
    lax.fori_loop(0, n, wait, 0, unroll=DMA_UNROLL)


def _dispatch_kernel(dest_ref, xn_ref, init_hbm, xb_hbm, sem):
    del init_hbm
    n = dest_ref.shape[2]
    _row_dma_loop(n, lambda j: pltpu.make_async_copy(
        xn_ref.at[pl.ds(j >> 1, 1)], xb_hbm.at[pl.ds(dest_ref[0, 0, j], 1)], sem))


def _dispatch(xn, dest, rows):
    N, D = xn.shape
    tm = dest.shape[2] // TOP_IN_GROUP
    return pl.pallas_call(
        _dispatch_kernel,
        grid=(N // tm,),
        in_specs=[pl.BlockSpec((1, 1, TOP_IN_GROUP * tm), lambda i: (i, 0, 0), memory_space=pltpu.SMEM),
                  pl.BlockSpec((tm, D), lambda i: (i, 0)), pl.BlockSpec(memory_space=pl.ANY)],
        out_specs=pl.BlockSpec(memory_space=pl.ANY),
        out_shape=jax.ShapeDtypeStruct((rows, D), F32),
        scratch_shapes=[pltpu.SemaphoreType.DMA(())],
        input_output_aliases={2: 0},
        compiler_params=_cparams("arbitrary"),
        name="moe_dispatch",
    )(dest, xn, jnp.zeros((rows, D), F32))


def _expert_kernel(be_ref, nact_ref, xb_ref, wu_ref, wd_ref, yb_ref):
    del be_ref
    active = pl.program_id(0) < nact_ref[0]

    @pl.when(active)
    def _():
        ab = _mm(xb_ref[...], wu_ref)
        a, b = ab[:, 0:D_EXPERT], ab[:, D_EXPERT:2 * D_EXPERT]
        yb_ref[...] = _mm(a * jax.nn.sigmoid(a) * b, wd_ref)

    @pl.when(jnp.logical_not(active))
    def _():
        yb_ref[...] = jnp.zeros(yb_ref.shape, F32)


def _experts(xb, blk_e, n_act, w_up, w_down):
    rows, D = xb.shape
    n_blk = rows // FFN_BLOCK
    P = w_up.shape[0]
    return pl.pallas_call(
        _expert_kernel,
        grid_spec=pltpu.PrefetchScalarGridSpec(
            num_scalar_prefetch=2,
            grid=(n_blk,),
            in_specs=[pl.BlockSpec((FFN_BLOCK, D), lambda i, be, na: (i, 0)),
                      pl.BlockSpec((P, None, D, 2 * D_EXPERT), lambda i, be, na: (0, be[i], 0, 0)),
                      pl.BlockSpec((P, None, D_EXPERT, D), lambda i, be, na: (0, be[i], 0, 0))],
            out_specs=pl.BlockSpec((FFN_BLOCK, D), lambda i, be, na: (i, 0)),
        ),
        out_shape=jax.ShapeDtypeStruct((rows, D), F32),
        compiler_params=_cparams("arbitrary"),
        name="moe_experts",
    )(blk_e, n_act, xb, w_up, w_down)


def _combine_kernel(dest_ref, dnext_ref, x_ref, route_ref, yb_hbm, o_ref, ybuf, sem):
    i = pl.program_id(0)
    tm = x_ref.shape[0]
    slot = i % 2

    def rows(dref, s, k):
        return lambda t: pltpu.make_async_copy(yb_hbm.at[pl.ds(dref[0, 0, TOP_IN_GROUP * t + k], 1)],
                                               ybuf.at[s, k, pl.ds(t, 1)], sem.at[s, k])

    def start_tile(dref, s):
        for k in range(TOP_IN_GROUP):
            copy = rows(dref, s, k)
            lax.fori_loop(0, tm, lambda t, c: (copy(t).start(), c)[1], 0, unroll=DMA_UNROLL)

    @pl.when(i == 0)
    def _():
        start_tile(dest_ref, 0)

    @pl.when(i + 1 < pl.num_programs(0))
    def _():
        start_tile(dnext_ref, 1 - slot)

    for k in range(TOP_IN_GROUP):
        copy = rows(dest_ref, slot, k)
        lax.fori_loop(0, tm, lambda t, c: (copy(t).wait(), c)[1], 0, unroll=DMA_UNROLL)
    route = route_ref[...]
    o_ref[...] = x_ref[...] + (route[:, 2:3] * ybuf[slot, 0] + route[:, 3:4] * ybuf[slot, 1])


def _combine(x1, route, yb, dest):
    N, D = x1.shape
    tm = dest.shape[2] // TOP_IN_GROUP
    n = N // tm
    dspec = lambda f: pl.BlockSpec((1, 1, TOP_IN_GROUP * tm), f, memory_space=pltpu.SMEM)
    return pl.pallas_call(
        _combine_kernel,
        grid=(n,),
        in_specs=[dspec(lambda i: (i, 0, 0)), dspec(lambda i: (jnp.minimum(i + 1, n - 1), 0, 0)),
                  pl.BlockSpec((tm, D), lambda i: (i, 0)), pl.BlockSpec((tm, LANES), lambda i: (i, 0)),
                  pl.BlockSpec(memory_space=pl.ANY)],
        out_specs=pl.BlockSpec((tm, D), lambda i: (i, 0)),
        out_shape=jax.ShapeDtypeStruct((N, D), F32),
        scratch_shapes=[pltpu.VMEM((2, TOP_IN_GROUP, tm, D), F32), pltpu.SemaphoreType.DMA((2, TOP_IN_GROUP))],
        compiler_params=_cparams("arbitrary"),
        name="moe_combine",
    )(dest, dest, x1, route, yb)


def _ffn(x1, xn, route, w):
    N, D = xn.shape
    M = N * TOP_IN_GROUP
    rank, cnt = _rank(route)
    counts = cnt[0, :N_EXPERTS].astype(jnp.int32)
    padded = (counts + FFN_BLOCK - 1) // FFN_BLOCK * FFN_BLOCK
    pend = jnp.cumsum(padded)
    pstart = pend - padded
    n_blk = -(-M // FFN_BLOCK) + N_EXPERTS
    e_id = route[:, 0:TOP_IN_GROUP].astype(jnp.int32)
    onehot = e_id[:, :, None] == jnp.arange(N_EXPERTS, dtype=jnp.int32)
    dest = jnp.sum(jnp.where(onehot, pstart, 0), axis=-1) + rank[:, 0:TOP_IN_GROUP].astype(jnp.int32)
    tm = min(ROW_TILE, N)
    dest = dest.reshape(N // tm, 1, TOP_IN_GROUP * tm)
    blk_start = jnp.arange(n_blk, dtype=jnp.int32) * FFN_BLOCK
    blk_e = jnp.minimum(jnp.sum(pend[None, :] <= blk_start[:, None], axis=1), N_EXPERTS - 1).astype(jnp.int32)
    n_act = (pend[-1] // FFN_BLOCK).astype(jnp.int32).reshape(1)
    xb = _dispatch(xn, dest, n_blk * FFN_BLOCK)
    yb = _experts(xb, blk_e, n_act, w['w_up'], w['w_down'])
    return _combine(x1, route, yb, dest)


def kernel(x_prompt, x_sample, cache_cmp_kv, cache_sel_kv, state_win_kv, state_conv, cache_mem_kv, page_table, mem_prompt, norm_mix, norm_ffn, w_in, q_gain, k_gain, w_cmp1, w_cmp2, pos_cmp, conv_w, conv_b, conv_ln_g, conv_ln_b, mem_norm, w_mem_kv, mem_qk_gain, w_branch, w_gate, b_gate, w_out, w_grp, b_grp, w_route, b_route, w_up, w_down):
    params = dict(norm_mix=norm_mix, norm_ffn=norm_ffn, w_in=w_in, q_gain=q_gain, k_gain=k_gain, w_cmp1=w_cmp1,
                  w_cmp2=w_cmp2, pos_cmp=pos_cmp, conv_w=conv_w, conv_b=conv_b, conv_ln_g=conv_ln_g,
                  conv_ln_b=conv_ln_b, mem_norm=mem_norm, w_mem_kv=w_mem_kv, mem_qk_gain=mem_qk_gain,
                  w_branch=w_branch, w_gate=w_gate, b_gate=b_gate, w_out=w_out, w_grp=w_grp, b_grp=b_grp,
                  w_route=w_route, b_route=b_route, w_up=w_up, w_down=w_down)
    B, T, D = x_prompt.shape
    DB, DS, _ = x_sample.shape
    depth = w_in.shape[0]
    R = T_SAMPLE_PAD
    xp = x_prompt.reshape(B * T, D)
    xs = jnp.pad(x_sample, ((0, 0), (0, R - DS), (0, 0))).reshape(DB * R, D)
    slot_minor = lambda a: jnp.transpose(a, (0, 1, 3, 4, 5, 2)).reshape(a.shape[0], a.shape[1], 256, a.shape[2])
    cmp_t, sel_t, win_t = slot_minor(cache_cmp_kv), slot_minor(cache_sel_kv), slot_minor(state_win_kv)
    outs = [[] for _ in range(9)]
    for l in range(depth):
        w = _layer_weights(l, params)
        xp, p_outs = _prompt_layer(xp, (B, T), mem_prompt, w)
        precise = l < depth - 1
        xs, s_outs = _sample_layer(l, xs, (DB, DS), cmp_t, sel_t, win_t, state_win_kv[l], state_conv[l],
                                   cache_mem_kv[l], page_table, _precise_weights(l, params, w) if precise else w, precise)
        for dst, o in zip(outs, p_outs + s_outs):
            dst.append(o)
    y_prompt = xp.reshape(B, T, D)
    y_sample = xs.reshape(DB, R, D)[:, :DS]
    return (y_prompt, y_sample) + tuple(jnp.stack(o) for o in outs)


def _kv5(a, nb, t):
    return a.reshape(nb, t, 2, N_KV, HEAD_DIM)


def _prompt_layer(xp, bt, mem_prompt, w):
    B, T = bt
    q, kvc, kvs, kvs_b, kvw, kvw_b, gates, u, qx = _proj_in(xp, w['nm'], w['w_in'], w['qg'], w['kg_sel'], w['kg_win'],
                                                            w['xq_gain'])
    n_pg = B * T // PAGE_SIZE
    fs = _cmp_matmul(kvc.reshape(n_pg, 8, 4096), jnp.arange(n_pg, dtype=jnp.int32), w['w1big'])
    kc, vc = _cmp_finish(fs, B, w['pos2'], w['w1f_hi'], w['w1f_lo'], w['w2big'], w['kg_cmp'])
    r3 = lambda a: a.reshape(B, T, a.shape[-1])
    o_nsa = _nsa_prompt(r3(q), r3(gates), kc, vc, r3(kvs_b), r3(kvw_b))
    o_conv = _conv_prompt(r3(u), w['conv_w'], w['conv_b'], w['ln_g'], w['ln_b'])
    m_kv = _mem_kv(mem_prompt, w['mem_norm'], w['w_mem'], w['xk_gain'])
    o_mem = _mem_attend(r3(qx), m_kv, 512)
    x1, xn, route = _tail(xp, o_nsa.reshape(B * T, 512), o_conv.reshape(B * T, 512), o_mem.reshape(B * T, 512),
                          w['nm'], w['wb'], w['wg'], w['bg'], w['wo'], w['nf'], w['wr_hi'], w['wr_lo'], w['br'])
    n_win = min(WINDOW, T)
    outs = [_kv5(kvc, B, T), _kv5(kvs, B, T), _kv5(kvw, B, T)[:, T - n_win:], r3(u)[:, T - (CONV_W - 1):],
            m_kv.reshape(B, -1, 2, X_HEADS, X_HEAD_DIM)]
    return _ffn(x1, xn, route, w), outs


def _sample_layer(l, xs, bs, cmp_t, sel_t, win_t, win_state, conv_state, mem_kv, page_table, w, precise=False):
    DB, DS = bs
    R = T_SAMPLE_PAD
    D = xs.shape[1]
    q, kvc, kvs, _, kvw, _, gates, u, qx = _proj_in(xs, w['nm'], w['w_in'], w['qg'], w['kg_sel'], w['kg_win'], w['xq_gain'])
    s3 = lambda a: a.reshape(DB, R, a.shape[-1])
    fs = _cmp_matmul_t(cmp_t, l, page_table.reshape(-1), w['w1kv'])
    kc, vc = _cmp_finish(fs, DB, w['pos2'], w['w1f_hi'], w['w1f_lo'], w['w2big'], w['kg_cmp'])
    o_nsa = _nsa_sample(s3(q), s3(gates), kc, vc, s3(kvs), win_t, s3(kvw), sel_t, l, page_table, precise)
    u_ext = jnp.concatenate([conv_state, s3(u)[:, :DS]], axis=1)
    o_conv = _conv_sample(u_ext.transpose(1, 0, 2), w['conv_w'], w['conv_b'], w['ln_g'], w['ln_b'], DS)
    o_conv = jnp.pad(o_conv.transpose(1, 0, 2), ((0, 0), (0, R - DS), (0, 0))).reshape(DB * R, CONV_CH)
    o_mem = _mem_attend(s3(qx), mem_kv.reshape(DB, -1, 1024), R, precise)
    x1, xn, route = _tail(xs, o_nsa.reshape(DB * R, 512), o_conv, o_mem.reshape(DB * R, 512),
                          w['nm'], w['wb'], w['wg'], w['bg'], w['wo'], w['nf'], w['wr_hi'], w['wr_lo'], w['br'])
    real = lambda a: a.reshape(DB, R, -1)[:, :DS].reshape(DB * DS, -1)
    x2 = _ffn(real(x1), real(xn), real(route), w)
    xs = jnp.pad(x2.reshape(DB, DS, D), ((0, 0), (0, R - DS), (0, 0))).reshape(DB * R, D)
    win_all = jnp.concatenate([win_state, _kv5(kvw, DB, R)[:, :DS]], axis=1)
    outs = [_kv5(kvc, DB, R)[:, :DS], _kv5(kvs, DB, R)[:, :DS], win_all[:, DS:], u_ext[:, -(CONV_W - 1):]]
    return xs, outs
```

```python
import functools

import numpy as np
import jax
import jax.numpy as jnp
from jax import lax
from jax.experimental import pallas as pl
from jax.experimental.pallas import tpu as pltpu

F32 = jnp.float32
BF16 = jnp.bfloat16

D_MODEL = 1024
N_HEADS = 8
HEAD_DIM = 64
N_KV = 2
Q_PER_KV = 4
L_CMP = 32
S_CMP = 16
CMP_HIDDEN = 128
L_SEL = 64
N_SEL_TOP = 16
WINDOW = 512
Q_BLOCK = 128
FORCE_BONUS = 1.0e4
CONV_CH = 512
CONV_W = 31
X_HEADS = 4
X_HEAD_DIM = 128
N_GROUPS = 4
EXP_PER_GROUP = 8
N_EXPERTS = 32
TOP_IN_GROUP = 2
D_EXPERT = 512
PAGE_SIZE = 128
EPS = 1e-6

LANES = 128
NEG = -1e30
TINY = float(np.finfo(np.float32).tiny)
VMEM_LIMIT = 56 * 1024 * 1024

C_Q, C_KVC, C_KVS, C_KVW, C_GLU_A, C_GLU_B, C_QX, C_GATE, C_END = 0, 512, 768, 1024, 1280, 1792, 2304, 2816, 2944

TM_PROJ = 512
TM_TAIL = 512
SEL_CHUNK = 1024
CMP_PAGES = 32
SAMPLE_PAGES = 16
T_SAMPLE_PAD = 8


def _cparams(*sem):
    return pltpu.CompilerParams(dimension_semantics=sem, vmem_limit_bytes=VMEM_LIMIT)


def _dot(a, b):
    return jnp.dot(a, b, preferred_element_type=F32)


def _dot_nt(a, b):
    return lax.dot_general(a, b, (((1,), (1,)), ((), ())), preferred_element_type=F32)


def _split(a):
    hi = a.astype(BF16)
    lo = (a - hi.astype(F32)).astype(BF16)
    return hi, lo


def _dot_hl(a, s):
    hi, lo = _split(a)
    return _dot(hi, s) + _dot(lo, s)


def _dot3(a, b, nt=False):
    dot = _dot_nt if nt else _dot
    ah, al = _split(a)
    bh, bl = _split(b)
    m = a.shape[0]
    top = dot(jnp.concatenate([ah, al], axis=0), bh)
    return top[:m] + top[m:] + dot(ah, bl)


def _mm(a, w_ref, rows=slice(None)):
    if w_ref.shape[0] == 1:
        return _dot(a.astype(BF16), w_ref[0, rows])
    ah, al = _split(a)
    m = a.shape[0]
    top = _dot(jnp.concatenate([ah, al], axis=0), w_ref[0, rows])
    return top[:m] + top[m:] + _dot(ah, w_ref[1, rows])


def _const_spec(shape):
    return pl.BlockSpec(shape, lambda *_: (0,) * len(shape), pipeline_mode=pl.Buffered(1))


def _rms(x, g):
    return x * lax.rsqrt(jnp.mean(x * x, axis=-1, keepdims=True) + EPS) * g


def _segnorm64(v, seg, segt, gain):
    ss = _dot_hl(v * v, seg)
    inv = lax.rsqrt(ss * (1.0 / HEAD_DIM) + EPS)
    return v * _dot_hl(inv, segt) * gain


def _masked_softmax(s, mask):
    sm = jnp.where(mask, s, NEG)
    m = jnp.max(sm, axis=-1, keepdims=True)
    e = jnp.where(mask, jnp.exp(sm - m), 0.0)
    return e / jnp.maximum(jnp.sum(e, axis=-1, keepdims=True), TINY)


def _iota(shape, dim):
    return lax.broadcasted_iota(jnp.int32, shape, dim)


def _seg_mats(width):
    n = width // HEAD_DIM
    s = np.zeros((width, LANES), np.float32)
    s[np.arange(width), np.arange(width) // HEAD_DIM] = 1.0
    assert n <= LANES
    return jnp.asarray(s, BF16), jnp.asarray(s.T.copy(), BF16)


def _proj_in_kernel(x_ref, nm_ref, w_ref, qg_ref, kgs_ref, kgw_ref, xg_ref, s8_ref, s8t_ref, s2_ref, s2t_ref,
                    q_ref, kvc_ref, kvs_ref, kvsb_ref, kvw_ref, kvwb_ref, gate_ref, u_ref, qx_ref):
    y = _mm(_rms(x_ref[...], nm_ref[...]), w_ref)
    q_ref[...] = _segnorm64(y[:, C_Q:C_KVC], s8_ref[...], s8t_ref[...], qg_ref[...])
    kvc_ref[...] = y[:, C_KVC:C_KVS]
    for c0, gain_ref, o_ref, ob_ref in ((C_KVS, kgs_ref, kvs_ref, kvsb_ref), (C_KVW, kgw_ref, kvw_ref, kvwb_ref)):
        kn = _segnorm64(y[:, c0:c0 + 128], s2_ref[...], s2t_ref[...], gain_ref[...])
        v = y[:, c0 + 128:c0 + 256]
        o_ref[:, 0:128] = kn
        o_ref[:, 128:256] = v
        ob_ref[:, 0:128] = kn.astype(BF16)
        ob_ref[:, 128:256] = v.astype(BF16)
    gate_ref[...] = jax.nn.sigmoid(y[:, C_GATE:C_END])
    u_ref[...] = y[:, C_GLU_A:C_GLU_B] * jax.nn.sigmoid(y[:, C_GLU_B:C_QX])
    xg = xg_ref[...]
    for h in range(X_HEADS):
        seg = y[:, C_QX + 128 * h:C_QX + 128 * (h + 1)]
        qx_ref[:, 128 * h:128 * (h + 1)] = _rms(seg, xg)


def _proj_in(x, nm, w, qg, kgs, kgw, xg):
    M = x.shape[0]
    tm = min(TM_PROJ, M)
    s8, s8t = _seg_mats(512)
    s2, s2t = _seg_mats(128)
    row = lambda i: (i, 0)
    fix = lambda i: (0, 0)
    widths = [(512, F32), (256, F32), (256, F32), (256, BF16), (256, F32), (256, BF16), (128, F32), (512, F32), (512, F32)]
    return pl.pallas_call(
        _proj_in_kernel,
        grid=(M // tm,),
        in_specs=[pl.BlockSpec((tm, D_MODEL), row), pl.BlockSpec((1, D_MODEL), fix),
                  _const_spec(w.shape), pl.BlockSpec((1, 512), fix), pl.BlockSpec((1, 128), fix),
                  pl.BlockSpec((1, 128), fix), pl.BlockSpec((1, 128), fix),
                  pl.BlockSpec((512, LANES), fix), pl.BlockSpec((LANES, 512), fix),
                  pl.BlockSpec((128, LANES), fix), pl.BlockSpec((LANES, 128), fix)],
        out_specs=[pl.BlockSpec((tm, wd), row) for wd, _ in widths],
        out_shape=[jax.ShapeDtypeStruct((M, wd), dt) for wd, dt in widths],
        compiler_params=_cparams("parallel"),
        name="proj_in",
    )(x, nm, w, qg, kgs, kgw, xg, s8, s8t, s2, s2t)


def _cmp_matmul_kernel(pt_ref, *refs):
    del pt_ref
    page_refs, w_ref, o_ref = refs[:-2], refs[-2], refs[-1]
    rows = jnp.concatenate([r[...] for r in page_refs], axis=0).astype(BF16)
    o_ref[...] = _dot(rows, w_ref[...])


def _cmp_matmul(pages, page_ids, w1big):
    n = page_ids.shape[0]
    P = CMP_PAGES
    page_spec = lambda k: pl.BlockSpec((None, 8, 4096), lambda i, pt: (pt[i * P + k], 0, 0))
    return pl.pallas_call(
        _cmp_matmul_kernel,
        grid_spec=pltpu.PrefetchScalarGridSpec(
            num_scalar_prefetch=1,
            grid=(n // P,),
            in_specs=[page_spec(k) for k in range(P)] + [pl.BlockSpec((4096, 1024), lambda i, pt: (0, 0))],
            out_specs=pl.BlockSpec((P * 8, 1024), lambda i, pt: (i, 0)),
        ),
        out_shape=jax.ShapeDtypeStruct((n * 8, 1024), F32),
        compiler_params=_cparams("arbitrary"),
        name="cmp_matmul",
    )(page_ids, *([pages] * P), w1big)


def _cmp_matmul_t_kernel(pt_ref, *refs):
    del pt_ref
    P = CMP_PAGES
    page_refs, w_ref, o_ref, rows_ref = refs[:P], refs[P], refs[P + 1], refs[P + 2]
    for k, r in enumerate(page_refs):
        rows = r[...].T
        for half in range(2):
            rows_ref[half, k * PAGE_SIZE:(k + 1) * PAGE_SIZE, :] = rows[:, half * LANES:(half + 1) * LANES]
    n_chunk = P * PAGE_SIZE // S_CMP
    outs = []
    for half in range(2):
        lhs = jnp.concatenate([rows_ref[half, pl.ds(p, n_chunk, stride=S_CMP), :] for p in range(S_CMP)], axis=1)
        outs.append(_mm(lhs, w_ref.at[half]))
    o_ref[...] = jnp.concatenate([outs[0][:, 0:256], outs[1][:, 0:256], outs[0][:, 256:512], outs[1][:, 256:512]], axis=1)


def _cmp_matmul_t(pages_t, layer, page_ids, w1kv):
    n = page_ids.shape[0]
    P = CMP_PAGES
    page_spec = lambda k: pl.BlockSpec((None, None, 256, PAGE_SIZE), lambda i, pt: (layer, pt[i * P + k], 0, 0))
    return pl.pallas_call(
        _cmp_matmul_t_kernel,
        grid_spec=pltpu.PrefetchScalarGridSpec(
            num_scalar_prefetch=1,
            grid=(n // P,),
            in_specs=[page_spec(k) for k in range(P)] + [_const_spec(w1kv.shape)],
            out_specs=pl.BlockSpec((P * 8, 1024), lambda i, pt: (i, 0)),
            scratch_shapes=[pltpu.VMEM((2, P * PAGE_SIZE, LANES), F32)],
        ),
        out_shape=jax.ShapeDtypeStruct((n * 8, 1024), F32),
        compiler_params=_cparams("arbitrary"),
        name="cmp_matmul_t",
    )(page_ids, *([pages_t] * P), w1kv)


def _gelu_tanh(x):
    return 0.5 * x * (1.0 + jnp.tanh(np.float32(np.sqrt(2.0 / np.pi)) * (x + 0.044715 * (x * x * x))))


def _cmp_finish_kernel(fs_ref, pos_ref, w1f_ref, w1fl_ref, w2_ref, kg_ref, s2_ref, s2t_ref, kc_ref, vc_ref):
    nc = fs_ref.shape[0]
    ph, plo = _split(pos_ref[...])
    pb = _dot(ph, w1f_ref[...]) + _dot(plo, w1f_ref[...]) + _dot(ph, w1fl_ref[...])
    lane = _iota((1, 512), 1)
    pos_bias = jnp.where(lane < 256, jnp.tile(pb[0:1], (1, 4)), jnp.tile(pb[1:2], (1, 4)))
    first = fs_ref[:, 0:512]
    second = pltpu.roll(fs_ref[:, 512:1024], nc - 1, 0)
    hid = _gelu_tanh(first + second + pos_bias)
    out = _mm(hid, w2_ref)
    kc_ref[...] = _segnorm64(out[:, 0:128], s2_ref[...], s2t_ref[...], kg_ref[...])
    vc_ref[...] = out[:, 128:256]


def _cmp_finish(fs, nb, pos2, w1flat, w1flat_lo, w2big, kg):
    nc = fs.shape[0] // nb
    s2, s2t = _seg_mats(128)
    fix = lambda b: (0, 0)
    return pl.pallas_call(
        _cmp_finish_kernel,
        grid=(nb,),
        in_specs=[pl.BlockSpec((nc, 1024), lambda b: (b, 0)), pl.BlockSpec((8, 4096), fix),
                  pl.BlockSpec((4096, 128), fix), pl.BlockSpec((4096, 128), fix), _const_spec(w2big.shape),
                  pl.BlockSpec((1, 128), fix), pl.BlockSpec((128, LANES), fix), pl.BlockSpec((LANES, 128), fix)],
        out_specs=[pl.BlockSpec((None, nc, 128), lambda b: (b, 0, 0))] * 2,
        out_shape=[jax.ShapeDtypeStruct((nb, nc, 128), F32)] * 2,
        compiler_params=_cparams("parallel"),
        name="cmp_finish",
    )(fs, pos2, w1flat, w1flat_lo, w2big, kg, s2, s2t)


def _qpad_group(q, g, dtype=BF16):
    lane = _iota((q.shape[0], LANES), 1)
    keep = (lane < HEAD_DIM) if g == 0 else (lane >= HEAD_DIM)
    parts = []
    for r in range(Q_PER_KV):
        h = Q_PER_KV * g + r
        x = q[:, 128 * (h // 2):128 * (h // 2 + 1)]
        if h % 2 != g:
            x = pltpu.roll(x, HEAD_DIM, 1)
        parts.append(jnp.where(keep, x * (HEAD_DIM ** -0.5), 0.0))
    return jnp.concatenate(parts, axis=0).astype(dtype)


def _slopes(rows_per_head, g):
    r = _iota((Q_PER_KV * rows_per_head, 1), 0) >> (rows_per_head.bit_length() - 1)
    out = jnp.zeros(r.shape, F32)
    for k in range(Q_PER_KV):
        out = jnp.where(r == k, np.float32(2.0 ** (-(Q_PER_KV * g + k + 1))), out)
    return out


def _tile4(x):
    return jnp.concatenate([x] * Q_PER_KV, axis=0)


def _top_select(score, n_top):
    lane = _iota(score.shape, 1).astype(F32)
    big = np.float32(score.shape[1])
    sel = jnp.zeros(score.shape, F32)
    sc = score
    for _ in range(n_top):
        m = jnp.max(sc, axis=-1, keepdims=True)
        idx = jnp.min(jnp.where(sc == m, lane, big), axis=-1, keepdims=True)
        pick = lane == idx
        sel = jnp.where(pick, 1.0, sel)
        sc = jnp.where(pick, -3.0e38, sc)
    return sel


def _select_blocks(p, cover, qpos, n_blk_real):
    R = qpos.shape[0]
    psum = p[0:R] + p[R:2 * R] + p[2 * R:3 * R] + p[3 * R:4 * R]
    imp = _dot_hl(psum, cover)
    blk = _iota(imp.shape, 1)
    cur = qpos >> 6
    valid = (blk * L_SEL <= qpos) & (blk < n_blk_real)
    forced = (blk == 0) | (blk == cur) | (blk == cur - 1)
    score = jnp.where(valid, imp + jnp.where(forced, FORCE_BONUS, 0.0), -FORCE_BONUS)
    score = jnp.where(blk < n_blk_real, score, -2.0e38)
    sel = _top_select(score, N_SEL_TOP)
    return jnp.where(valid, sel, 0.0)


def _merge_heads(gates, oc, os_, ow, R):
    lane = _iota((R, LANES), 1)
    tiles = []
    for t in range(N_HEADS // 2):
        g = t // 2
        halves = []
        for h in (2 * t, 2 * t + 1):
            r = h % Q_PER_KV
            sl = slice(r * R, (r + 1) * R)
            o = (gates[:, 3 * h:3 * h + 1] * oc[g][sl] + gates[:, 3 * h + 1:3 * h + 2] * os_[g][sl]
                 + gates[:, 3 * h + 2:3 * h + 3] * ow[g][sl])
            halves.append(o)
        lo, hi = halves
        if g == 0:
            hi = pltpu.roll(hi, HEAD_DIM, 1)
        else:
            lo = pltpu.roll(lo, HEAD_DIM, 1)
        tiles.append(jnp.where(lane < HEAD_DIM, lo, hi))
    return jnp.concatenate(tiles, axis=1)


MASK_BIAS = -1e30
SEL_BLOCKS = SEL_CHUNK // L_SEL
X_HI, X_LO, X_CHUNK, X_ROW = SEL_BLOCKS, SEL_BLOCKS + 1, SEL_BLOCKS + 2, SEL_BLOCKS + 3
X_ONE = X_CHUNK


def _ext_cols(lane4, cols):
    out = jnp.zeros(lane4.shape, F32)
    for k, v in cols:
        out = jnp.where(lane4 == k, v, out)
    return out


def _top_select_t(score, n_top):
    row = _iota(score.shape, 0).astype(F32)
    big = np.float32(score.shape[0])
    sel = jnp.zeros(score.shape, F32)
    sc = score
    for _ in range(n_top):
        m = jnp.max(sc, axis=0, keepdims=True)
        idx = jnp.min(jnp.where(sc == m, row, big), axis=0, keepdims=True)
        pick = row == idx
        sel = jnp.where(pick, 1.0, sel)
        sc = jnp.where(pick, -3.0e38, sc)
    return sel


def _nsa_prompt_kernel(q_ref, g_ref, kc_ref, vc_ref, kvs_ref, kvw_ref, covt_ref, xc_ref, xs_ref, xw_ref, o_ref):
    R = Q_BLOCK
    i = pl.program_id(1)
    start = i * R
    startf = start.astype(F32)
    q = q_ref[...]
    qloc = _iota((R, 1), 0)
    n_cmp = kc_ref.shape[0]
    nb = covt_ref.shape[0]
    last = (start + R - 1) // SEL_CHUNK
    k_last = pl.multiple_of(last * SEL_CHUNK, SEL_CHUNK)
    ws = pl.multiple_of(jnp.maximum(start - WINDOW, 0), R)
    woff = start - ws
    bias_c = jnp.where(start + qloc - (_iota((1, n_cmp), 1) * S_CMP + (L_CMP - 1)) >= 0, 0.0, MASK_BIAS)
    dw = woff + qloc - _iota((1, WINDOW + R), 1)
    bias_w = jnp.where((dw >= 0) & (dw < WINDOW), 0.0, MASK_BIAS)
    bias_d = jnp.where(start + qloc - (k_last + _iota((1, SEL_CHUNK), 1)) >= 0, 0.0, MASK_BIAS)
    has_cmp = jnp.where(start + qloc >= L_CMP - 1, 1.0, 0.0)
    qloc4 = _tile4(qloc.astype(F32))
    lane4 = _iota((Q_PER_KV * R, LANES), 1)
    blk = _iota((nb, 1), 0)
    qpos_row = start + _iota((1, R), 1)
    cur = qpos_row >> 6
    valid_t = blk * L_SEL <= qpos_row
    forced_t = (blk == 0) | (blk == cur) | (blk == cur - 1)
    HR = Q_PER_KV * R
    qg = jnp.concatenate([_qpad_group(q, 0), _qpad_group(q, 1)], axis=0)
    slope = jnp.concatenate([_slopes(R, 0), _slopes(R, 1)], axis=0)
    qloc8 = jnp.concatenate([qloc4, qloc4], axis=0)
    lane8 = jnp.concatenate([lane4, lane4], axis=0)

    ext = _ext_cols(lane8, [(0, slope * 1024.0), (1, slope * 16.0), (2, -slope * startf),
                            (3, -slope * (qloc8 - (L_CMP - 1.0)))])
    lhs = jnp.concatenate([qg, ext.astype(BF16)], axis=1)
    s = _dot_nt(lhs, jnp.concatenate([kc_ref[...].astype(BF16), xc_ref[...]], axis=1))
    ps, psums = [], []
    for h in range(N_HEADS):
        sr = s[h * R:(h + 1) * R] + bias_c
        e = jnp.exp(sr - jnp.max(sr, axis=-1, keepdims=True))
        p = e * (has_cmp / jnp.sum(e, axis=-1, keepdims=True))
        ps.append(p.astype(BF16))
        if h % Q_PER_KV == 0:
            psums.append(p)
        else:
            psums[-1] = psums[-1] + p
    oc = _dot(jnp.concatenate(ps, axis=0), vc_ref[...].astype(BF16))
    selb = []
    for g in range(N_KV):
        ph, plo = _split(psums[g])
        imp_t = _dot_nt(covt_ref[...], ph) + _dot_nt(covt_ref[...], plo)
        score = jnp.where(valid_t, imp_t + jnp.where(forced_t, FORCE_BONUS, 0.0), -FORCE_BONUS)
        sel_t = _top_select_t(score, N_SEL_TOP)
        selb.append(jnp.where(valid_t & (sel_t > 0.5), 0.0, MASK_BIAS).T)

    kvw = kvw_ref[pl.ds(ws, WINDOW + R), :]
    ext = _ext_cols(lane8, [(X_HI, slope * 256.0), (X_LO, slope), (X_CHUNK, -slope * woff.astype(F32)),
                            (X_ROW, -slope * qloc8)])
    lhs = jnp.concatenate([qg, ext.astype(BF16)], axis=1)
    s = _dot_nt(lhs, jnp.concatenate([kvw[:, 0:128], xw_ref[...]], axis=1))
    es = []
    for h in range(N_HEADS):
        sr = s[h * R:(h + 1) * R] + bias_w
        es.append(jnp.exp(sr - jnp.max(sr, axis=-1, keepdims=True)).astype(BF16))
    acc = _dot(jnp.concatenate(es, axis=0), jnp.concatenate([kvw[:, 128:256], xw_ref[...]], axis=1))
    ow = acc[:, 0:LANES] / acc[:, LANES + X_ONE:LANES + X_ONE + 1]

    ext_base = _ext_cols(lane8, [(X_HI, slope * 256.0), (X_LO, slope), (X_ROW, -slope * qloc8)])

    def scores(c):
        k0 = pl.multiple_of(c * SEL_CHUNK, SEL_CHUNK)
        shift = lax.rem(nb - SEL_BLOCKS * c, nb)
        sel8 = jnp.concatenate([_tile4(pltpu.roll(selb[0], shift, 1)), _tile4(pltpu.roll(selb[1], shift, 1))], axis=0)
        ext_l = jnp.where(lane8 < SEL_BLOCKS, sel8,
                          jnp.where(lane8 == X_CHUNK, slope * (k0 - start).astype(F32), ext_base))
        lhs = jnp.concatenate([qg, ext_l.astype(BF16)], axis=1)
        return _dot_nt(lhs, jnp.concatenate([kvs_ref[pl.ds(k0, SEL_CHUNK), 0:128], xs_ref[...]], axis=1))

    def absorb(c, sc, m, acc, diag):
        k0 = pl.multiple_of(c * SEL_CHUNK, SEL_CHUNK)
        es, ms = [], []
        for h in range(N_HEADS):
            rows = slice(h * R, (h + 1) * R)
            sr = sc[rows]
            if diag:
                sr = sr + bias_d
            m_new = jnp.maximum(m[rows], jnp.max(sr, axis=-1, keepdims=True))
            es.append(jnp.exp(sr - m_new).astype(BF16))
            ms.append(m_new)
        m_new = jnp.concatenate(ms, axis=0)
        v_aug = jnp.concatenate([kvs_ref[pl.ds(k0, SEL_CHUNK), 128:256], xs_ref[...]], axis=1)
        return m_new, jnp.exp(m - m_new) * acc + _dot(jnp.concatenate(es, axis=0), v_aug)

    def body(c, carry):
        sc, m, acc = carry
        sc_next = scores(c + 1)
        m, acc = absorb(c, sc, m, acc, False)
        return sc_next, m, acc

    init = (scores(0), jnp.full((2 * HR, 1), MASK_BIAS, F32), jnp.zeros((2 * HR, 2 * LANES), F32))
    sc, m, acc = lax.fori_loop(0, last, body, init)
    _, acc = absorb(last, sc, m, acc, True)
    os_ = acc[:, 0:LANES] / acc[:, LANES + X_ONE:LANES + X_ONE + 1]
    split = lambda a: [a[0:HR], a[HR:2 * HR]]
    o_ref[...] = _merge_heads(g_ref[...], split(oc), split(os_), split(ow), R)


def _pos_ext(n, sel_shift=None):
    j = np.arange(n)
    x = np.zeros((n, LANES), np.float32)
    if sel_shift is not None:
        x[j, (j >> sel_shift) % SEL_BLOCKS] = 1.0
    x[:, X_HI] = j >> 8
    x[:, X_LO] = j & 255
    x[:, X_CHUNK] = 1.0
    x[:, X_ROW] = 1.0
    return jnp.asarray(x, BF16)


def _nsa_prompt(q, gates, kc, vc, kvs_b, kvw_b):
    B, T, _ = q.shape
    n_cmp = kc.shape[1]
    n_sel = T // L_SEL
    assert n_sel <= LANES and T % SEL_CHUNK == 0
    start = np.arange(n_cmp)[None, :] * S_CMP
    j = np.arange(LANES)[:, None]
    cover_t = ((start < (j + 1) * L_SEL) & (start + L_CMP > j * L_SEL) & (np.arange(n_cmp)[None, :] < T // S_CMP - 1)
               & (j < n_sel))
    n = np.arange(n_cmp)
    xc = np.zeros((n_cmp, LANES), np.float32)
    xc[:, 0], xc[:, 1], xc[:, 2], xc[:, 3] = n >> 6, n & 63, 1.0, 1.0
    blk = lambda b, i: (b, i, 0)
    per_b = lambda b, i: (b, 0, 0)
    fix = lambda b, i: (0, 0)
    return pl.pallas_call(
        _nsa_prompt_kernel,
        grid=(B, T // Q_BLOCK),
        in_specs=[pl.BlockSpec((None, Q_BLOCK, 512), blk), pl.BlockSpec((None, Q_BLOCK, 128), blk),
                  pl.BlockSpec((None, n_cmp, 128), per_b), pl.BlockSpec((None, n_cmp, 128), per_b),
                  pl.BlockSpec((None, T, 256), per_b), pl.BlockSpec((None, T, 256), per_b),
                  pl.BlockSpec((LANES, n_cmp), fix), pl.BlockSpec((n_cmp, LANES), fix),
                  pl.BlockSpec((SEL_CHUNK, LANES), fix), pl.BlockSpec((WINDOW + Q_BLOCK, LANES), fix)],
        out_specs=pl.BlockSpec((None, Q_BLOCK, 512), blk),
        out_shape=jax.ShapeDtypeStruct((B, T, 512), F32),
        compiler_params=_cparams("parallel", "arbitrary"),
        name="nsa_prompt",
    )(q, gates, kc, vc, kvs_b, kvw_b, jnp.asarray(cover_t, BF16), jnp.asarray(xc, BF16),
      _pos_ext(SEL_CHUNK, 6), _pos_ext(WINDOW + Q_BLOCK))


def _nsa_sample_kernel(pt_ref, q_ref, g_ref, kc_ref, vc_ref, cov_ref, kvn_ref, win_ref, kwn_ref, *rest,
                       past, n_blk_real, precise):
    del pt_ref
    cast = (lambda a: a) if precise else (lambda a: a.astype(BF16))
    mm = _dot3 if precise else _dot
    mm_nt = functools.partial(_dot3, nt=True) if precise else _dot_nt
    P = SAMPLE_PAGES
    page_refs = rest[:P]
    o_ref = rest[P]
    qg_s, selm_s, m_s, l_s, acc_s, oc_s, ow_s = rest[P + 1:]
    R = T_SAMPLE_PAD
    HR = Q_PER_KV * R
    j = pl.program_id(1)
    nj = pl.num_programs(1)
    qpos = past + _iota((R, 1), 0)
    qpos8 = jnp.concatenate([_tile4(qpos)] * N_KV, axis=0)
    slope = jnp.concatenate([_slopes(R, g) for g in range(N_KV)], axis=0)
    n_cmp = kc_ref.shape[0]
    nb = cov_ref.shape[1]
    chunk = P * PAGE_SIZE
    zpad = jnp.zeros((LANES - R, 256), F32)

    @pl.when(j == 0)
    def _():
        q = q_ref[...]
        qg = jnp.concatenate([_qpad_group(q, g, qg_s.dtype) for g in range(N_KV)], axis=0)
        qg_s[...] = qg
        ncol = _iota((1, n_cmp), 1)
        d = qpos8 - (ncol * S_CMP + (L_CMP - 1))
        s = mm_nt(qg, cast(kc_ref[...])) - slope * d.astype(F32)
        p = _masked_softmax(s, (d >= 0) & (ncol < n_cmp - 1))
        oc_s[...] = mm(cast(p), cast(vc_ref[...]))
        for g in range(N_KV):
            selm_s[g] = _select_blocks(p[g * HR:(g + 1) * HR], cov_ref[...], qpos, n_blk_real)
        win = cast(win_ref[...])
        col = _iota((1, WINDOW), 1)
        d_old = qpos8 - (past - WINDOW + col)
        s_old = jnp.where(d_old < WINDOW, mm(qg, win[0:128]) - slope * d_old.astype(F32), NEG)
        kv_new = cast(jnp.concatenate([kwn_ref[...], zpad], axis=0))
        coln = _iota((1, LANES), 1)
        d_new = qpos8 - (past + coln)
        s_new = jnp.where((d_new >= 0) & (coln < 4), mm_nt(qg, kv_new[:, 0:128]) - slope * d_new.astype(F32), NEG)
        m = jnp.maximum(jnp.max(s_old, axis=-1, keepdims=True), jnp.max(s_new, axis=-1, keepdims=True))
        e_old = jnp.where(d_old < WINDOW, jnp.exp(s_old - m), 0.0)
        e_new = jnp.where((d_new >= 0) & (coln < 4), jnp.exp(s_new - m), 0.0)
        den = jnp.sum(e_old, axis=-1, keepdims=True) + jnp.sum(e_new, axis=-1, keepdims=True)
        num = mm_nt(cast(e_old), win[128:256]) + mm(cast(e_new), kv_new[:, 128:256])
        ow_s[...] = num / jnp.maximum(den, TINY)
        m_s[...] = jnp.full(m_s.shape, NEG, F32)
        l_s[...] = jnp.zeros(l_s.shape, F32)
        acc_s[...] = jnp.zeros(acc_s.shape, F32)

    def update(scores, values, kpos, extra_mask):
        nk = kpos.shape[1]
        expand = jnp.where(_iota((nb, nk), 0) == (kpos >> 6), 1.0, 0.0).astype(BF16)
        picked = jnp.concatenate([_tile4(_dot(selm_s[g].astype(BF16), expand)) for g in range(N_KV)], axis=0)
        dd = qpos8 - kpos
        mask = (picked > 0.5) & (dd >= 0)
        if extra_mask is not None:
            mask = mask & extra_mask
        sc = jnp.where(mask, scores(qg_s[...]) - slope * dd.astype(F32), NEG)
        m = m_s[...]
        m_new = jnp.maximum(m, jnp.max(sc, axis=-1, keepdims=True))
        a = jnp.exp(m - m_new)
        e = jnp.where(mask, jnp.exp(sc - m_new), 0.0)
        m_s[...] = m_new
        l_s[...] = a * l_s[...] + jnp.sum(e, axis=-1, keepdims=True)
        acc_s[...] = a * acc_s[...] + values(cast(e))

    k_t = jnp.concatenate([cast(r[0:128, :]) for r in page_refs], axis=1)
    v_t = jnp.concatenate([cast(r[128:256, :]) for r in page_refs], axis=1)
    update(lambda qg: mm(qg, k_t), lambda e: mm_nt(e, v_t), j * chunk + _iota((1, chunk), 1), None)

    @pl.when(j == nj - 1)
    def _():
        kv_new = cast(jnp.concatenate([kvn_ref[...], zpad], axis=0))
        col = _iota((1, LANES), 1)
        update(lambda qg: mm_nt(qg, kv_new[:, 0:128]), lambda e: mm(e, kv_new[:, 128:256]), past + col, col < 4)
        os_ = acc_s[...] / jnp.maximum(l_s[...], TINY)
        split = lambda a: [a[0:HR], a[HR:2 * HR]]
        o_ref[...] = _merge_heads(g_ref[...], split(oc_s[...]), split(os_), split(ow_s[...]), R)


def _nsa_sample(q, gates, kc, vc, kvs_new, win_t, kvw_new, sel_pages_t, layer, page_table, precise=False):
    DB = q.shape[0]
    n_pages = page_table.shape[1]
    past = n_pages * PAGE_SIZE
    P = SAMPLE_PAGES
    R = T_SAMPLE_PAD
    n_cmp = kc.shape[1]
    n_blk_real = -(-(past + 4) // L_SEL)
    nb = -(-n_blk_real // LANES) * LANES
    start = np.arange(n_cmp)[:, None] * S_CMP
    jj = np.arange(nb)[None, :]
    cover = ((start < (jj + 1) * L_SEL) & (start + L_CMP > jj * L_SEL) & (jj < n_blk_real)
             & (np.arange(n_cmp)[:, None] < n_cmp - 1))
    per_b = lambda b, j, pt: (b, 0, 0)
    page_spec = lambda k: pl.BlockSpec((None, None, 256, PAGE_SIZE),
                                       lambda b, j, pt: (layer, pt[b * n_pages + j * P + k], 0, 0))
    kern = functools.partial(_nsa_sample_kernel, past=past, n_blk_real=n_blk_real, precise=precise)
    rows = N_KV * Q_PER_KV * R
    return pl.pallas_call(
        kern,
        grid_spec=pltpu.PrefetchScalarGridSpec(
            num_scalar_prefetch=1,
            grid=(DB, n_pages // P),
            in_specs=[pl.BlockSpec((None, R, 512), per_b), pl.BlockSpec((None, R, 128), per_b),
                      pl.BlockSpec((None, n_cmp, 128), per_b), pl.BlockSpec((None, n_cmp, 128), per_b),
                      pl.BlockSpec((n_cmp, nb), lambda b, j, pt: (0, 0)),
                      pl.BlockSpec((None, R, 256), per_b),
                      pl.BlockSpec((None, None, 256, WINDOW), lambda b, j, pt: (layer, b, 0, 0)),
                      pl.BlockSpec((None, R, 256), per_b)] + [page_spec(k) for k in range(P)],
            out_specs=pl.BlockSpec((None, R, 512), per_b),
            scratch_shapes=[pltpu.VMEM((rows, LANES), F32 if precise else BF16), pltpu.VMEM((N_KV, R, nb), F32),
                            pltpu.VMEM((rows, 1), F32), pltpu.VMEM((rows, 1), F32),
                            pltpu.VMEM((rows, LANES), F32), pltpu.VMEM((rows, LANES), F32),
                            pltpu.VMEM((rows, LANES), F32)],
        ),
        out_shape=jax.ShapeDtypeStruct((DB, R, 512), F32),
        compiler_params=_cparams("parallel", "arbitrary"),
        name="nsa_sample",
    )(page_table.reshape(-1), q, gates, kc, vc, jnp.asarray(cover, BF16), kvs_new, win_t, kvw_new,
      *([sel_pages_t] * P))


def _ln_silu(c, g, b):
    mu = jnp.mean(c, axis=-1, keepdims=True)
    xc = c - mu
    y = xc * lax.rsqrt(jnp.mean(xc * xc, axis=-1, keepdims=True) + EPS) * g + b
    return y * jax.nn.sigmoid(y)


def _conv_prompt_kernel(um_ref, up_ref, w_ref, b_ref, g_ref, bb_ref, o_ref, ext_ref):
    tt = um_ref.shape[0]
    i = pl.program_id(1)
    ext_ref[0:32, :] = jnp.where(i > 0, up_ref[...], 0.0)
    ext_ref[32:32 + tt, :] = um_ref[...]
    acc = jnp.zeros((tt, CONV_CH), F32)
    for j in range(CONV_W):
        acc = acc + ext_ref[pl.ds(j + 2, tt), :] * w_ref[j:j + 1, :]
    o_ref[...] = _ln_silu(acc + b_ref[...], g_ref[...], bb_ref[...])


def _conv_prompt(u, w, b, g, bb):
    B, T, C = u.shape
    tt = 512
    fix = lambda bi, i: (0, 0)
    return pl.pallas_call(
        _conv_prompt_kernel,
        grid=(B, T // tt),
        in_specs=[pl.BlockSpec((None, tt, C), lambda bi, i: (bi, i, 0)),
                  pl.BlockSpec((None, 32, C), lambda bi, i: (bi, jnp.maximum(i * (tt // 32) - 1, 0), 0)),
                  pl.BlockSpec((32, C), fix), pl.BlockSpec((1, C), fix), pl.BlockSpec((1, C), fix),
                  pl.BlockSpec((1, C), fix)],
        out_specs=pl.BlockSpec((None, tt, C), lambda bi, i: (bi, i, 0)),
        out_shape=jax.ShapeDtypeStruct((B, T, C), F32),
        scratch_shapes=[pltpu.VMEM((tt + 32, C), F32)],
        compiler_params=_cparams("parallel", "arbitrary"),
        name="conv_prompt",
    )(u, u, w, b, g, bb)


def _conv_sample_kernel(ext_ref, w_ref, b_ref, g_ref, bb_ref, o_ref):
    n_out = o_ref.shape[0]
    for t in range(n_out):
        acc = jnp.zeros(o_ref.shape[1:], F32)
        for j in range(CONV_W):
            acc = acc + ext_ref[t + j] * w_ref[j:j + 1, :]
        o_ref[t] = _ln_silu(acc + b_ref[...], g_ref[...], bb_ref[...])


def _conv_sample(ext_t, w, b, g, bb, n_out):
    _, DB, C = ext_t.shape
    return pl.pallas_call(
        _conv_sample_kernel,
        out_shape=jax.ShapeDtypeStruct((n_out, DB, C), F32),
        name="conv_sample",
    )(ext_t, w, b, g, bb)


def _mem_kv_kernel(m_ref, g_ref, w_ref, kg_ref, o_ref):
    hb = _rms(m_ref[...], g_ref[...]).astype(BF16)
    kv = _dot(hb, w_ref[...])
    kg = kg_ref[...]
    for h in range(X_HEADS):
        o_ref[:, 128 * h:128 * (h + 1)] = _rms(kv[:, 128 * h:128 * (h + 1)], kg)
    o_ref[:, 512:1024] = kv[:, 512:1024]


def _mem_kv(mem, g, w, kg):
    B, M, D = mem.shape
    fix = lambda b: (0, 0)
    return pl.pallas_call(
        _mem_kv_kernel,
        grid=(B,),
        in_specs=[pl.BlockSpec((None, M, D), lambda b: (b, 0, 0)), pl.BlockSpec((1, D), fix),
                  pl.BlockSpec((D, 1024), fix), pl.BlockSpec((1, 128), fix)],
        out_specs=pl.BlockSpec((None, M, 1024), lambda b: (b, 0, 0)),
        out_shape=jax.ShapeDtypeStruct((B, M, 1024), F32),
        compiler_params=_cparams("parallel"),
        name="mem_kv",
    )(mem, g, w, kg)


def _mem_attend_kernel(q_ref, kv_ref, o_ref, *, precise):
    for h in range(X_HEADS):
        q = q_ref[:, 128 * h:128 * (h + 1)]
        k = kv_ref[:, 128 * h:128 * (h + 1)]
        v = kv_ref[:, 512 + 128 * h:512 + 128 * (h + 1)]
        s = (_dot3(q, k, nt=True) if precise else _dot_nt(q.astype(BF16), k.astype(BF16))) * (X_HEAD_DIM ** -0.5)
        e = jnp.exp(s - jnp.max(s, axis=-1, keepdims=True))
        p = e / jnp.sum(e, axis=-1, keepdims=True)
        o_ref[:, 128 * h:128 * (h + 1)] = _dot3(p, v) if precise else _dot(p.astype(BF16), v.astype(BF16))


def _mem_attend(qx, mkv, tq, precise=False):
    B, T, _ = qx.shape
    M = mkv.shape[1]
    return pl.pallas_call(
        functools.partial(_mem_attend_kernel, precise=precise),
        grid=(B, T // tq),
        in_specs=[pl.BlockSpec((None, tq, 512), lambda b, i: (b, i, 0)),
                  pl.BlockSpec((None, M, 1024), lambda b, i: (b, 0, 0))],
        out_specs=pl.BlockSpec((None, tq, 512), lambda b, i: (b, i, 0)),
        out_shape=jax.ShapeDtypeStruct((B, T, 512), F32),
        compiler_params=_cparams("parallel", "arbitrary"),
        name="mem_attend",
    )(qx, mkv)


def _tail_kernel(x_ref, on_ref, ocv_ref, om_ref, nm_ref, wb_ref, wg_ref, bg_ref, wo_ref, nf_ref,
                 wrh_ref, wrl_ref, br_ref, x1_ref, xn_ref, route_ref):
    x = x_ref[...]
    gate = jax.nn.sigmoid(_mm(_rms(x, nm_ref[...]), wg_ref) + bg_ref[...])
    y_nsa = _mm(on_ref[...], wb_ref, slice(0, 512))
    y_conv = _mm(ocv_ref[...], wb_ref, slice(512, 1024))
    y_mem = _mm(om_ref[...], wb_ref, slice(1024, 1536))
    merged = gate[:, 0:1024] * y_nsa + gate[:, 1024:2048] * y_conv + gate[:, 2048:3072] * y_mem
    x1 = x + _mm(merged, wo_ref)
    x1_ref[...] = x1
    xn = _rms(x1, nf_ref[...])
    xn_ref[...] = xn
    xh, xl = _split(xn)
    logits = _dot(xh, wrh_ref[...]) + _dot(xl, wrh_ref[...]) + _dot(xh, wrl_ref[...]) + br_ref[...]
    lane = _iota(logits.shape, 1)
    lanef = lane.astype(F32)
    is_g = lane < N_GROUPS
    gl = jnp.where(is_g, logits, NEG)
    gm = jnp.max(gl, axis=-1, keepdims=True)
    p_g = 1.0 / jnp.sum(jnp.where(is_g, jnp.exp(gl - gm), 0.0), axis=-1, keepdims=True)
    g_sel = jnp.min(jnp.where(gl == gm, lanef, 128.0), axis=-1, keepdims=True).astype(jnp.int32)
    lo = N_GROUPS + EXP_PER_GROUP * g_sel
    el = jnp.where((lane >= lo) & (lane < lo + EXP_PER_GROUP), logits, NEG)
    m0 = jnp.max(el, axis=-1, keepdims=True)
    i0 = jnp.min(jnp.where(el == m0, lanef, 128.0), axis=-1, keepdims=True)
    el = jnp.where(lanef == i0, 2.0 * NEG, el)
    m1 = jnp.max(el, axis=-1, keepdims=True)
    i1 = jnp.min(jnp.where(el == m1, lanef, 128.0), axis=-1, keepdims=True)
    t = jnp.exp(m1 - m0)
    w0 = p_g / (1.0 + t)
    w1 = p_g * t / (1.0 + t)
    route_ref[...] = jnp.where(lane == 0, i0 - N_GROUPS, jnp.where(lane == 1, i1 - N_GROUPS,
                               jnp.where(lane == 2, w0, jnp.where(lane == 3, w1, 0.0))))


def _tail(x, o_nsa, o_conv, o_mem, nm, wb, wg, bg, wo, nf, wrh, wrl, br):
    M = x.shape[0]
    tm = min(TM_TAIL, M)
    row = lambda i: (i, 0)
    fix = lambda i: (0, 0)
    return pl.pallas_call(
        _tail_kernel,
        grid=(M // tm,),
        in_specs=[pl.BlockSpec((tm, D_MODEL), row), pl.BlockSpec((tm, 512), row), pl.BlockSpec((tm, 512), row),
                  pl.BlockSpec((tm, 512), row), pl.BlockSpec((1, D_MODEL), fix), _const_spec(wb.shape),
                  _const_spec(wg.shape), pl.BlockSpec((1, 3072), fix), _const_spec(wo.shape),
                  pl.BlockSpec((1, D_MODEL), fix), pl.BlockSpec((D_MODEL, LANES), fix), pl.BlockSpec((D_MODEL, LANES), fix),
                  pl.BlockSpec((1, LANES), fix)],
        out_specs=[pl.BlockSpec((tm, D_MODEL), row), pl.BlockSpec((tm, D_MODEL), row), pl.BlockSpec((tm, LANES), row)],
        out_shape=[jax.ShapeDtypeStruct((M, D_MODEL), F32), jax.ShapeDtypeStruct((M, D_MODEL), F32),
                   jax.ShapeDtypeStruct((M, LANES), F32)],
        compiler_params=_cparams("parallel"),
        name="tail",
    )(x, o_nsa, o_conv, o_mem, nm, wb, wg, bg, wo, nf, wrh, wrl, br)


def _stack(w, precise):
    hi = w.astype(BF16)
    if not precise:
        return hi[None]
    return jnp.stack([hi, (w - hi.astype(F32)).astype(BF16)])


def _cmp_weights_t(w1, precise):
    eye = jnp.eye(N_KV, dtype=F32)
    halves = []
    for kv in range(2):
        w = w1[kv].reshape(2, S_CMP, HEAD_DIM, CMP_HIDDEN)
        big = jnp.einsum('spdf,gG->pgdsGf', w, eye).reshape(S_CMP * N_KV * HEAD_DIM, 2 * N_KV * CMP_HIDDEN)
        halves.append(_stack(big, precise))
    return jnp.stack(halves)


def _precise_weights(l, p, w):
    out = dict(w)
    out.update(w_in=_stack(w['w_in_f32'], True), wb=_stack(p['w_branch'][l], True), wg=_stack(p['w_gate'][l], True),
               wo=_stack(p['w_out'][l], True), w_up=_stack(p['w_up'][l], True), w_down=_stack(p['w_down'][l], True),
               w1kv=_cmp_weights_t(p['w_cmp1'][l], True), w2big=_stack(w['w2big_f32'], True))
    return out


def _layer_weights(l, p):
    w = p['w_in'][l]
    n_gate = 3 * N_HEADS
    w_re = jnp.concatenate([w[:, :1280], w[:, 1280 + n_gate:], w[:, 1280:1280 + n_gate],
                            jnp.zeros((D_MODEL, C_END - C_GATE - n_gate), F32)], axis=1)
    w1 = p['w_cmp1'][l]
    eye = jnp.eye(2, dtype=F32)
    halves = []
    for half in range(2):
        wh = w1[:, S_CMP * half:S_CMP * (half + 1)]
        big = jnp.einsum('kpdf,kK,gG->pkgdKGf', wh, eye, eye)
        halves.append(big.reshape(S_CMP * 256, 4 * CMP_HIDDEN))
    w1big = jnp.concatenate(halves, axis=1).astype(BF16)
    w1flat = jnp.concatenate([w1[0].reshape(L_CMP * HEAD_DIM, CMP_HIDDEN), w1[1].reshape(L_CMP * HEAD_DIM, CMP_HIDDEN)], axis=0)
    pos = p['pos_cmp'][l].reshape(2, L_CMP * HEAD_DIM)
    pos2 = jnp.zeros((8, 4096), F32).at[0, :2048].set(pos[0]).at[1, 2048:].set(pos[1])
    w1f_hi = w1flat.astype(BF16)
    w1f_lo = (w1flat - w1f_hi.astype(F32)).astype(BF16)
    w2 = p['w_cmp2'][l]
    w2big_f32 = jnp.einsum('kfd,kK,gG->kgfKGd', w2, eye, eye).reshape(4 * CMP_HIDDEN, 256)
    w2big = _stack(w2big_f32, False)
    wr = jnp.concatenate([p['w_grp'][l], p['w_route'][l], jnp.zeros((D_MODEL, LANES - N_GROUPS - N_EXPERTS), F32)], axis=1)
    wr_hi = wr.astype(BF16)
    wr_lo = (wr - wr_hi.astype(F32)).astype(BF16)
    br = jnp.concatenate([p['b_grp'][l], p['b_route'][l], jnp.zeros((LANES - N_GROUPS - N_EXPERTS,), F32)]).reshape(1, LANES)
    cw = jnp.concatenate([p['conv_w'][l], jnp.zeros((1, CONV_CH), F32)], axis=0)
    r1 = lambda a: a.reshape(1, -1)
    return dict(
        nm=r1(p['norm_mix'][l]), w_in=_stack(w_re, False), w_in_f32=w_re, qg=r1(jnp.tile(p['q_gain'][l], N_HEADS)),
        kg_cmp=r1(jnp.tile(p['k_gain'][l, 0], 2)), kg_sel=r1(jnp.tile(p['k_gain'][l, 1], 2)),
        kg_win=r1(jnp.tile(p['k_gain'][l, 2], 2)), xq_gain=r1(p['mem_qk_gain'][l, 0]), xk_gain=r1(p['mem_qk_gain'][l, 1]),
        w1big=w1big, w1kv=_cmp_weights_t(w1, False), pos2=pos2, w1f_hi=w1f_hi, w1f_lo=w1f_lo, w2big=w2big,
        w2big_f32=w2big_f32,
        conv_w=cw, conv_b=r1(p['conv_b'][l]), ln_g=r1(p['conv_ln_g'][l]), ln_b=r1(p['conv_ln_b'][l]),
        mem_norm=r1(p['mem_norm'][l]), w_mem=p['w_mem_kv'][l].astype(BF16),
        wb=_stack(p['w_branch'][l], False), wg=_stack(p['w_gate'][l], False), bg=r1(p['b_gate'][l]),
        wo=_stack(p['w_out'][l], False), nf=r1(p['norm_ffn'][l]), wr_hi=wr_hi, wr_lo=wr_lo, br=br,
        w_up=_stack(p['w_up'][l], False), w_down=_stack(p['w_down'][l], False),
    )


FFN_BLOCK = 256
ROW_TILE = 256
DMA_UNROLL = 8


def _rank_kernel(route_ref, tri_ref, rank_ref, cnt_ref, base_ref):
    @pl.when(pl.program_id(0) == 0)
    def _():
        base_ref[...] = jnp.zeros(base_ref.shape, F32)

    route = route_ref[...]
    lane = _iota(route.shape, 1)
    lanef = lane.astype(F32)
    o0 = jnp.where(lanef == route[:, 0:1], 1.0, 0.0)
    o1 = jnp.where(lanef == route[:, 1:2], 1.0, 0.0)
    both = o0 + o1
    before = _dot(tri_ref[...], both.astype(BF16)) + base_ref[0:1, :]
    r0 = jnp.sum(o0 * before, axis=-1, keepdims=True)
    r1 = jnp.sum(o1 * (before + o0), axis=-1, keepdims=True)
    rank_ref[...] = jnp.where(lane == 0, r0, jnp.where(lane == 1, r1, 0.0))
    total = base_ref[0:1, :] + jnp.sum(both, axis=0, keepdims=True)
    base_ref[...] = jnp.broadcast_to(total, base_ref.shape)
    cnt_ref[...] = jnp.broadcast_to(total, cnt_ref.shape)


def _rank(route):
    N = route.shape[0]
    tr = min(512, N)
    tri = jnp.asarray(np.tril(np.ones((tr, tr), np.float32), -1), BF16)
    return pl.pallas_call(
        _rank_kernel,
        grid=(N // tr,),
        in_specs=[pl.BlockSpec((tr, LANES), lambda i: (i, 0)), pl.BlockSpec((tr, tr), lambda i: (0, 0))],
        out_specs=[pl.BlockSpec((tr, LANES), lambda i: (i, 0)), pl.BlockSpec((8, LANES), lambda i: (0, 0))],
        out_shape=[jax.ShapeDtypeStruct((N, LANES), F32), jax.ShapeDtypeStruct((8, LANES), F32)],
        scratch_shapes=[pltpu.VMEM((8, LANES), F32)],
        compiler_params=_cparams("arbitrary"),
        name="moe_rank",
    )(route, tri)


def _row_dma_loop(n, make_copy):
    def start(j, c):
        make_copy(j).start()
        return c

    def wait(j, c):
        make_copy(j).wait()
        return c

    lax.fori_loop(0, n, start, 0, unroll=DMA_UNROLL)
    lax.fori_loop(0, n, wait, 0, unroll=DMA_UNROLL)


def _dispatch_kernel(dest_ref, xn_ref, init_hbm, xb_hbm, sem):
    del init_hbm
    n = dest_ref.shape[2]
    _row_dma_loop(n, lambda j: pltpu.make_async_copy(
        xn_ref.at[pl.ds(j >> 1, 1)], xb_hbm.at[pl.ds(dest_ref[0, 0, j], 1)], sem))


def _dispatch(xn, dest, rows):
    N, D = xn.shape
    tm = dest.shape[2] // TOP_IN_GROUP
    return pl.pallas_call(
        _dispatch_kernel,
        grid=(N // tm,),
        in_specs=[pl.BlockSpec((1, 1, TOP_IN_GROUP * tm), lambda i: (i, 0, 0), memory_space=pltpu.SMEM),
                  pl.BlockSpec((tm, D), lambda i: (i, 0)), pl.BlockSpec(memory_space=pl.ANY)],
        out_specs=pl.BlockSpec(memory_space=pl.ANY),
        out_shape=jax.ShapeDtypeStruct((rows, D), F32),
        scratch_shapes=[pltpu.SemaphoreType.DMA(())],
        input_output_aliases={2: 0},
        compiler_params=_cparams("arbitrary"),
        name="moe_dispatch",
    )(dest, xn, jnp.zeros((rows, D), F32))


def _expert_kernel(be_ref, nact_ref, xb_ref, wu_ref, wd_ref, yb_ref):
    del be_ref
    active = pl.program_id(0) < nact_ref[0]

    @pl.when(active)
    def _():
        ab = _mm(xb_ref[...], wu_ref)
        a, b = ab[:, 0:D_EXPERT], ab[:, D_EXPERT:2 * D_EXPERT]
        yb_ref[...] = _mm(a * jax.nn.sigmoid(a) * b, wd_ref)

    @pl.when(jnp.logical_not(active))
    def _():
        yb_ref[...] = jnp.zeros(yb_ref.shape, F32)


def _experts(xb, blk_e, n_act, w_up, w_down):
    rows, D = xb.shape
    n_blk = rows // FFN_BLOCK
    P = w_up.shape[0]
    return pl.pallas_call(
        _expert_kernel,
        grid_spec=pltpu.PrefetchScalarGridSpec(
            num_scalar_prefetch=2,
            grid=(n_blk,),
            in_specs=[pl.BlockSpec((FFN_BLOCK, D), lambda i, be, na: (i, 0)),
                      pl.BlockSpec((P, None, D, 2 * D_EXPERT), lambda i, be, na: (0, be[i], 0, 0)),
                      pl.BlockSpec((P, None, D_EXPERT, D), lambda i, be, na: (0, be[i], 0, 0))],
            out_specs=pl.BlockSpec((FFN_BLOCK, D), lambda i, be, na: (i, 0)),
        ),
        out_shape=jax.ShapeDtypeStruct((rows, D), F32),
        compiler_params=_cparams("arbitrary"),
        name="moe_experts",
    )(blk_e, n_act, xb, w_up, w_down)


def _combine_kernel(dest_ref, dnext_ref, x_ref, route_ref, yb_hbm, o_ref, ybuf, sem):
    i = pl.program_id(0)
    tm = x_ref.shape[0]
    slot = i % 2

    def rows(dref, s, k):
        return lambda t: pltpu.make_async_copy(yb_hbm.at[pl.ds(dref[0, 0, TOP_IN_GROUP * t + k], 1)],
                                               ybuf.at[s, k, pl.ds(t, 1)], sem.at[s, k])

    def start_tile(dref, s):
        for k in range(TOP_IN_GROUP):
            copy = rows(dref, s, k)
            lax.fori_loop(0, tm, lambda t, c: (copy(t).start(), c)[1], 0, unroll=DMA_UNROLL)

    @pl.when(i == 0)
    def _():
        start_tile(dest_ref, 0)

    @pl.when(i + 1 < pl.num_programs(0))
    def _():
        start_tile(dnext_ref, 1 - slot)

    for k in range(TOP_IN_GROUP):
        copy = rows(dest_ref, slot, k)
        lax.fori_loop(0, tm, lambda t, c: (copy(t).wait(), c)[1], 0, unroll=DMA_UNROLL)
    route = route_ref[...]
    o_ref[...] = x_ref[...] + (route[:, 2:3] * ybuf[slot, 0] + route[:, 3:4] * ybuf[slot, 1])


def _combine(x1, route, yb, dest):
    N, D = x1.shape
    tm = dest.shape[2] // TOP_IN_GROUP
    n = N // tm
    dspec = lambda f: pl.BlockSpec((1, 1, TOP_IN_GROUP * tm), f, memory_space=pltpu.SMEM)
    return pl.pallas_call(
        _combine_kernel,
        grid=(n,),
        in_specs=[dspec(lambda i: (i, 0, 0)), dspec(lambda i: (jnp.minimum(i + 1, n - 1), 0, 0)),
                  pl.BlockSpec((tm, D), lambda i: (i, 0)), pl.BlockSpec((tm, LANES), lambda i: (i, 0)),
                  pl.BlockSpec(memory_space=pl.ANY)],
        out_specs=pl.BlockSpec((tm, D), lambda i: (i, 0)),
        out_shape=jax.ShapeDtypeStruct((N, D), F32),
        scratch_shapes=[pltpu.VMEM((2, TOP_IN_GROUP, tm, D), F32), pltpu.SemaphoreType.DMA((2, TOP_IN_GROUP))],
        compiler_params=_cparams("arbitrary"),
        name="moe_combine",
    )(dest, dest, x1, route, yb)


def _ffn(x1, xn, route, w):
    N, D = xn.shape
    M = N * TOP_IN_GROUP
    rank, cnt = _rank(route)
    counts = cnt[0, :N_EXPERTS].astype(jnp.int32)
    padded = (counts + FFN_BLOCK - 1) // FFN_BLOCK * FFN_BLOCK
    pend = jnp.cumsum(padded)
    pstart = pend - padded
    n_blk = -(-M // FFN_BLOCK) + N_EXPERTS
    e_id = route[:, 0:TOP_IN_GROUP].astype(jnp.int32)
    onehot = e_id[:, :, None] == jnp.arange(N_EXPERTS, dtype=jnp.int32)
    dest = jnp.sum(jnp.where(onehot, pstart, 0), axis=-1) + rank[:, 0:TOP_IN_GROUP].astype(jnp.int32)
    tm = min(ROW_TILE, N)
    dest = dest.reshape(N // tm, 1, TOP_IN_GROUP * tm)
    blk_start = jnp.arange(n_blk, dtype=jnp.int32) * FFN_BLOCK
    blk_e = jnp.minimum(jnp.sum(pend[None, :] <= blk_start[:, None], axis=1), N_EXPERTS - 1).astype(jnp.int32)
    n_act = (pend[-1] // FFN_BLOCK).astype(jnp.int32).reshape(1)
    xb = _dispatch(xn, dest, n_blk * FFN_BLOCK)
    yb = _experts(xb, blk_e, n_act, w['w_up'], w['w_down'])
    return _combine(x1, route, yb, dest)


def kernel(x_prompt, x_sample, cache_cmp_kv, cache_sel_kv, state_win_kv, state_conv, cache_mem_kv, page_table, mem_prompt, norm_mix, norm_ffn, w_in, q_gain, k_gain, w_cmp1, w_cmp2, pos_cmp, conv_w, conv_b, conv_ln_g, conv_ln_b, mem_norm, w_mem_kv, mem_qk_gain, w_branch, w_gate, b_gate, w_out, w_grp, b_grp, w_route, b_route, w_up, w_down):
    params = dict(norm_mix=norm_mix, norm_ffn=norm_ffn, w_in=w_in, q_gain=q_gain, k_gain=k_gain, w_cmp1=w_cmp1,
                  w_cmp2=w_cmp2, pos_cmp=pos_cmp, conv_w=conv_w, conv_b=conv_b, conv_ln_g=conv_ln_g,
                  conv_ln_b=conv_ln_b, mem_norm=mem_norm, w_mem_kv=w_mem_kv, mem_qk_gain=mem_qk_gain,
                  w_branch=w_branch, w_gate=w_gate, b_gate=b_gate, w_out=w_out, w_grp=w_grp, b_grp=b_grp,
                  w_route=w_route, b_route=b_route, w_up=w_up, w_down=w_down)
    B, T, D = x_prompt.shape
    DB, DS, _ = x_sample.shape
    depth = w_in.shape[0]
    R = T_SAMPLE_PAD
    xp = x_prompt.reshape(B * T, D)
    xs = jnp.pad(x_sample, ((0, 0), (0, R - DS), (0, 0))).reshape(DB * R, D)
    slot_minor = lambda a: jnp.transpose(a, (0, 1, 3, 4, 5, 2)).reshape(a.shape[0], a.shape[1], 256, a.shape[2])
    cmp_t, sel_t, win_t = slot_minor(cache_cmp_kv), slot_minor(cache_sel_kv), slot_minor(state_win_kv)
    outs = [[] for _ in range(9)]
    for l in range(depth):
        w = _layer_weights(l, params)
        xp, p_outs = _prompt_layer(xp, (B, T), mem_prompt, w)
        precise = l < depth - 1
        xs, s_outs = _sample_layer(l, xs, (DB, DS), cmp_t, sel_t, win_t, state_win_kv[l], state_conv[l],
                                   cache_mem_kv[l], page_table, _precise_weights(l, params, w) if precise else w, precise)
        for dst, o in zip(outs, p_outs + s_outs):
            dst.append(o)
    y_prompt = xp.reshape(B, T, D)
    y_sample = xs.reshape(DB, R, D)[:, :DS]
    return (y_prompt, y_sample) + tuple(jnp.stack(o) for o in outs)


def _kv5(a, nb, t):
    return a.reshape(nb, t, 2, N_KV, HEAD_DIM)


def _prompt_layer(xp, bt, mem_prompt, w):
    B, T = bt
    q, kvc, kvs, kvs_b, kvw, kvw_b, gates, u, qx = _proj_in(xp, w['nm'], w['w_in'], w['qg'], w['kg_sel'], w['kg_win'],
                                                            w['xq_gain'])
    n_pg = B * T // PAGE_SIZE
    fs = _cmp_matmul(kvc.reshape(n_pg, 8, 4096), jnp.arange(n_pg, dtype=jnp.int32), w['w1big'])
    kc, vc = _cmp_finish(fs, B, w['pos2'], w['w1f_hi'], w['w1f_lo'], w['w2big'], w['kg_cmp'])
    r3 = lambda a: a.reshape(B, T, a.shape[-1])
    o_nsa = _nsa_prompt(r3(q), r3(gates), kc, vc, r3(kvs_b), r3(kvw_b))
    o_conv = _conv_prompt(r3(u), w['conv_w'], w['conv_b'], w['ln_g'], w['ln_b'])
    m_kv = _mem_kv(mem_prompt, w['mem_norm'], w['w_mem'], w['xk_gain'])
    o_mem = _mem_attend(r3(qx), m_kv, 512)
    x1, xn, route = _tail(xp, o_nsa.reshape(B * T, 512), o_conv.reshape(B * T, 512), o_mem.reshape(B * T, 512),
                          w['nm'], w['wb'], w['wg'], w['bg'], w['wo'], w['nf'], w['wr_hi'], w['wr_lo'], w['br'])
    n_win = min(WINDOW, T)
    outs = [_kv5(kvc, B, T), _kv5(kvs, B, T), _kv5(kvw, B, T)[:, T - n_win:], r3(u)[:, T - (CONV_W - 1):],
            m_kv.reshape(B, -1, 2, X_HEADS, X_HEAD_DIM)]
    return _ffn(x1, xn, route, w), outs


def _sample_layer(l, xs, bs, cmp_t, sel_t, win_t, win_state, conv_state, mem_kv, page_table, w, precise=False):
    DB, DS = bs
    R = T_SAMPLE_PAD
    D = xs.shape[1]
    q, kvc, kvs, _, kvw, _, gates, u, qx = _proj_in(xs, w['nm'], w['w_in'], w['qg'], w['kg_sel'], w['kg_win'], w['xq_gain'])
    s3 = lambda a: a.reshape(DB, R, a.shape[-1])
    fs = _cmp_matmul_t(cmp_t, l, page_table.reshape(-1), w['w1kv'])
    kc, vc = _cmp_finish(fs, DB, w['pos2'], w['w1f_hi'], w['w1f_lo'], w['w2big'], w['kg_cmp'])
    o_nsa = _nsa_sample(s3(q), s3(gates), kc, vc, s3(kvs), win_t, s3(kvw), sel_t, l, page_table, precise)
    u_ext = jnp.concatenate([conv_state, s3(u)[:, :DS]], axis=1)
    o_conv = _conv_sample(u_ext.transpose(1, 0, 2), w['conv_w'], w['conv_b'], w['ln_g'], w['ln_b'], DS)
    o_conv = jnp.pad(o_conv.transpose(1, 0, 2), ((0, 0), (0, R - DS), (0, 0))).reshape(DB * R, CONV_CH)
    o_mem = _mem_attend(s3(qx), mem_kv.reshape(DB, -1, 1024), R, precise)
    x1, xn, route = _tail(xs, o_nsa.reshape(DB * R, 512), o_conv, o_mem.reshape(DB * R, 512),
                          w['nm'], w['wb'], w['wg'], w['bg'], w['wo'], w['nf'], w['wr_hi'], w['wr_lo'], w['br'])
    real = lambda a: a.reshape(DB, R, -1)[:, :DS].reshape(DB * DS, -1)
    x2 = _ffn(real(x1), real(xn), real(route), w)
    xs = jnp.pad(x2.reshape(DB, DS, D), ((0, 0), (0, R - DS), (0, 0))).reshape(DB * R, D)
    win_all = jnp.concatenate([win_state, _kv5(kvw, DB, R)[:, :DS]], axis=1)
    outs = [_kv5(kvc, DB, R)[:, :DS], _kv5(kvs, DB, R)[:, :DS], win_all[:, DS:], u_ext[:, -(CONV_W - 1):]]
    return xs, outs
```
